```python
import math
import jax, jax.numpy as jnp
from jax import lax
import numpy as np

D_MODEL = 2048
BATCH = 1
SEQ = 8192
DEPTH = 4

CHUNK = 64
Q_BLOCK = 128
D_MIX = D_MODEL

MLA_HEADS = 6
MLA_NOPE = 128
MLA_ROPE = 64
MLA_V = 128
MLA_Q_RANK = 512
MLA_KV_RANK = 256
MLA_WIDTH = MLA_HEADS * MLA_V
ROPE_THETA = 10000.0

SG_GROUPS = 4
SG_GROUP_CH = 128
SG_WIDTH = SG_GROUPS * SG_GROUP_CH
SG_CHUNK = 128

SB_HEADS = 4
SB_HEAD_DIM = 128
SB_WIDTH = SB_HEADS * SB_HEAD_DIM

MEM_TOKENS = 256
MEM_HEADS = 4
MEM_HEAD_DIM = 64
MEM_WIDTH = MEM_HEADS * MEM_HEAD_DIM

IN_SIZES = (MLA_Q_RANK, MLA_KV_RANK, MLA_ROPE, MLA_WIDTH,
            SG_WIDTH, SG_WIDTH, SG_WIDTH,
            SB_WIDTH, SB_WIDTH, SB_WIDTH, SB_WIDTH,
            MEM_WIDTH, MEM_WIDTH)
D_IN = sum(IN_SIZES)

DEEPNORM_ALPHA = (2.0 * DEPTH) ** 0.25
DEEPNORM_BETA = (8.0 * DEPTH) ** -0.25
LN_EPS = 1e-5
RMS_EPS = 1e-6

kernel_name = "hybrid_mla_gmlp_stickbreak_deepnorm"


def _layer_norm(x, g, b):
    xf = x.astype(jnp.float32)
    mu = jnp.mean(xf, axis=-1, keepdims=True)
    xc = xf - mu
    var = jnp.mean(xc * xc, axis=-1, keepdims=True)
    return (xc * lax.rsqrt(var + LN_EPS) * g.astype(jnp.float32) + b.astype(jnp.float32)).astype(x.dtype)


def _rms_norm(x, g):
    xf = x.astype(jnp.float32)
    ms = jnp.mean(xf * xf, axis=-1, keepdims=True)
    return (xf * lax.rsqrt(ms + RMS_EPS) * g.astype(jnp.float32)).astype(x.dtype)


def _rope(x, cos, sin):
    half = x.shape[-1] // 2
    x1, x2 = x[..., :half], x[..., half:]
    return jnp.concatenate([x1 * cos - x2 * sin, x1 * sin + x2 * cos], axis=-1)


def _sweep_query_blocks(block_fn, q):
    b, s, h, d = q.shape
    nb = s // Q_BLOCK
    qb = q.reshape(b, nb, Q_BLOCK, h, d).transpose(1, 0, 2, 3, 4)
    out = lax.map(block_fn, (qb, jnp.arange(nb, dtype=jnp.int32)))
    return out.transpose(1, 0, 2, 3, 4).reshape(b, s, h, out.shape[-1])


def _chunk_causal_softmax_attention(q, k, v, scale):
    key_chunk = jnp.arange(k.shape[1]) // CHUNK

    def block(args):
        qblk, i = args
        qpos = i * Q_BLOCK + jnp.arange(Q_BLOCK)
        logits = jnp.einsum('bqhd,bkhd->bhqk', qblk, k).astype(jnp.float32) * scale
        mask = key_chunk[None, :] <= (qpos // CHUNK)[:, None]
        p = jax.nn.softmax(jnp.where(mask, logits, -jnp.inf), axis=-1)
        return jnp.einsum('bhqk,bkhd->bqhd', p.astype(v.dtype), v)

    return _sweep_query_blocks(block, q)


def _stick_breaking_attention(q, k, v, scale):
    kpos = jnp.arange(k.shape[1])

    def block(args):
        qblk, i = args
        qpos = i * Q_BLOCK + jnp.arange(Q_BLOCK)
        z = jnp.einsum('bqhd,bkhd->bhqk', qblk, k).astype(jnp.float32) * scale
        strict = kpos[None, :] < qpos[:, None]
        log_beta = jax.nn.log_sigmoid(z)
        log_1mb = jnp.where(strict, jax.nn.log_sigmoid(-z), 0.0)
        rev = lax.cumsum(log_1mb, axis=3, reverse=True)
        log_a = log_beta + rev - log_1mb
        a = jnp.where(strict, jnp.exp(log_a), 0.0)
        return jnp.einsum('bhqk,bkhd->bqhd', a.astype(v.dtype), v)

    return _sweep_query_blocks(block, q)


def setup_inputs(seed: int = 0) -> dict:
    key = jax.random.key(seed)
    ks = jax.random.split(key, 20)
    f32 = jnp.float32
    nrm = lambda k, shape, s: jax.random.normal(k, shape, f32) * s
    x = jax.random.normal(ks[0], (BATCH, SEQ, D_MODEL), f32)
    mem = jax.random.normal(ks[1], (BATCH, MEM_TOKENS, D_MODEL), f32)
    offset = jax.random.randint(ks[2], (BATCH, 1), 0, 4096, dtype=jnp.int32)
    positions = (offset + jnp.arange(SEQ, dtype=jnp.int32)[None, :]).astype(jnp.int32)
    w_in = nrm(ks[3], (DEPTH, D_MODEL, D_IN), D_MODEL ** -0.5)
    q_norm_g = 1.0 + nrm(ks[4], (DEPTH, MLA_Q_RANK), 0.01)
    w_uq = nrm(ks[5], (DEPTH, MLA_Q_RANK, MLA_HEADS * (MLA_NOPE + MLA_ROPE)), MLA_Q_RANK ** -0.5)
    kv_norm_g = 1.0 + nrm(ks[6], (DEPTH, MLA_KV_RANK), 0.01)
    w_ukv = nrm(ks[7], (DEPTH, MLA_KV_RANK, MLA_HEADS * (MLA_NOPE + MLA_V)), MLA_KV_RANK ** -0.5)
    sg_ln_g = 1.0 + nrm(ks[8], (DEPTH, SG_WIDTH), 0.01)
    sg_ln_b = nrm(ks[9], (DEPTH, SG_WIDTH), 0.01)
    sg_w = nrm(ks[10], (DEPTH, SG_GROUPS, SG_CHUNK, SG_CHUNK), SG_CHUNK ** -0.5)
    sg_b = 1.0 + nrm(ks[11], (DEPTH, SG_GROUPS, SG_CHUNK), 0.01)
    w_mem_k = nrm(ks[12], (DEPTH, D_MODEL, MEM_WIDTH), D_MODEL ** -0.5)
    w_mem_v = nrm(ks[13], (DEPTH, D_MODEL, MEM_WIDTH), D_MODEL ** -0.5)
    w_out = nrm(ks[14], (DEPTH, D_MIX, D_MODEL), D_MIX ** -0.5 * DEEPNORM_BETA)
    ln_g = 1.0 + nrm(ks[15], (DEPTH, D_MODEL), 0.01)
    ln_b = nrm(ks[16], (DEPTH, D_MODEL), 0.01)
    return {"x": x, "mem": mem, "positions": positions, "w_in": w_in,
            "q_norm_g": q_norm_g, "w_uq": w_uq, "kv_norm_g": kv_norm_g, "w_ukv": w_ukv,
            "sg_ln_g": sg_ln_g, "sg_ln_b": sg_ln_b, "sg_w": sg_w, "sg_b": sg_b,
            "w_mem_k": w_mem_k, "w_mem_v": w_mem_v, "w_out": w_out,
            "ln_g": ln_g, "ln_b": ln_b}


def reference(x, mem, positions, w_in, q_norm_g, w_uq, kv_norm_g, w_ukv,
              sg_ln_g, sg_ln_b, sg_w, sg_b, w_mem_k, w_mem_v, w_out, ln_g, ln_b):
    b, s, _ = x.shape
    inv_freq = ROPE_THETA ** (-jnp.arange(0, MLA_ROPE, 2, dtype=jnp.float32) / MLA_ROPE)
    ang = positions.astype(jnp.float32)[..., None] * inv_freq[None, None, :]
    cos = jnp.cos(ang).astype(x.dtype)
    sin = jnp.sin(ang).astype(x.dtype)
    split_idx = [int(v) for v in np.cumsum(IN_SIZES)[:-1]]
    p_in = jnp.arange(SG_CHUNK) // CHUNK
    sg_mask = (p_in[None, :] <= p_in[:, None]).astype(x.dtype)
    mla_scale = 1.0 / math.sqrt(MLA_NOPE + MLA_ROPE)
    sb_scale = 1.0 / math.sqrt(SB_HEAD_DIM)
    mem_scale = 1.0 / math.sqrt(MEM_HEAD_DIM)

    for l in range(DEPTH):
        h = jnp.einsum('bsd,de->bse', x, w_in[l])
        (c_q, c_kv, k_pe, g_a, sg_u, sg_v, g_b, sb_q, sb_k, sb_v, g_c, m_q, g_m) = jnp.split(h, split_idx, axis=-1)

        q = jnp.einsum('bsr,re->bse', _rms_norm(c_q, q_norm_g[l]), w_uq[l]).reshape(b, s, MLA_HEADS, MLA_NOPE + MLA_ROPE)
        q = jnp.concatenate([q[..., :MLA_NOPE], _rope(q[..., MLA_NOPE:], cos[:, :, None, :], sin[:, :, None, :])], axis=-1)
        kv = jnp.einsum('bsr,re->bse', _rms_norm(c_kv, kv_norm_g[l]), w_ukv[l]).reshape(b, s, MLA_HEADS, MLA_NOPE + MLA_V)
        k_rot = jnp.broadcast_to(_rope(k_pe, cos, sin)[:, :, None, :], (b, s, MLA_HEADS, MLA_ROPE))
        k = jnp.concatenate([kv[..., :MLA_NOPE], k_rot], axis=-1)
        o_a = _chunk_causal_softmax_attention(q, k, kv[..., MLA_NOPE:], mla_scale).reshape(b, s, MLA_WIDTH)

        u = jax.nn.gelu(sg_u)
        vn = _layer_norm(jax.nn.gelu(sg_v), sg_ln_g[l], sg_ln_b[l])
        vn = vn.reshape(b, s // SG_CHUNK, SG_CHUNK, SG_GROUPS, SG_GROUP_CH)
        w_sp = sg_w[l] * sg_mask[None]
        mixed = jnp.einsum('gts,bnsgc->bntgc', w_sp, vn) + sg_b[l].T[None, None, :, :, None]
        o_b = u * mixed.reshape(b, s, SG_WIDTH)

        o_c = _stick_breaking_attention(sb_q.reshape(b, s, SB_HEADS, SB_HEAD_DIM),
                                        sb_k.reshape(b, s, SB_HEADS, SB_HEAD_DIM),
                                        sb_v.reshape(b, s, SB_HEADS, SB_HEAD_DIM), sb_scale).reshape(b, s, SB_WIDTH)

        mk = jnp.einsum('bmd,de->bme', mem, w_mem_k[l]).reshape(b, MEM_TOKENS, MEM_HEADS, MEM_HEAD_DIM)
        mv = jnp.einsum('bmd,de->bme', mem, w_mem_v[l]).reshape(b, MEM_TOKENS, MEM_HEADS, MEM_HEAD_DIM)
        mq = m_q.reshape(b, s, MEM_HEADS, MEM_HEAD_DIM)
        mp = jax.nn.softmax(jnp.einsum('bshe,bmhe->bhsm', mq, mk).astype(jnp.float32) * mem_scale, axis=-1)
        o_m = jnp.einsum('bhsm,bmhe->bshe', mp.astype(mv.dtype), mv).reshape(b, s, MEM_WIDTH)

        y = jnp.concatenate([o_a * jax.nn.silu(g_a), o_b * jax.nn.silu(g_b),
                             o_c * jax.nn.silu(g_c), o_m * jax.nn.silu(g_m)], axis=-1)
        y = jnp.einsum('bse,ed->bsd', y, w_out[l])

        x = _layer_norm(DEEPNORM_ALPHA * x + y, ln_g[l], ln_b[l])
    return x
```

```python
import functools
import math

import jax
import jax.numpy as jnp
from jax import lax
from jax.experimental import pallas as pl
from jax.experimental.pallas import tpu as pltpu

F32 = jnp.float32
BF16 = jnp.bfloat16

D_MODEL = 2048
CHUNK = 64
MLA_HEADS = 6
MLA_NOPE = 128
MLA_ROPE = 64
MLA_V = 128
MLA_Q_RANK = 512
MLA_KV_RANK = 256
MLA_WIDTH = MLA_HEADS * MLA_V
ROPE_THETA = 10000.0
SG_GROUPS = 4
SG_GROUP_CH = 128
SG_WIDTH = SG_GROUPS * SG_GROUP_CH
SG_CHUNK = 128
SB_HEADS = 4
SB_HEAD_DIM = 128
SB_WIDTH = SB_HEADS * SB_HEAD_DIM
MEM_TOKENS = 256
MEM_HEADS = 4
MEM_HEAD_DIM = 64
MEM_WIDTH = MEM_HEADS * MEM_HEAD_DIM
LN_EPS = 1e-5
RMS_EPS = 1e-6

MLA_SCALE = 1.0 / math.sqrt(MLA_NOPE + MLA_ROPE)
SB_SCALE = 1.0 / math.sqrt(SB_HEAD_DIM)
MEM_SCALE = 1.0 / math.sqrt(MEM_HEAD_DIM)

LANE = 128
MLA_QK_PAD = 256
NEG_BIG = -1e30
VMEM_LIMIT = 48 * 1024 * 1024

H1_LAT = 0
H1_LAT_W = 1024
H1_SGU = 1024
H1_SGV = 1536
H1_GB = 2048
H1_GA = 2560
H1_GC = 3328
H1_GM = 3840
H1_W = 4096
H2_SBQ = 0
H2_SBK = 512
H2_SBV = 1024
H2_MQ = 1536
H2_W = 1792


def _cparams(*sem):
    return pltpu.CompilerParams(dimension_semantics=sem, vmem_limit_bytes=VMEM_LIMIT)


def _rope_table_kernel(pos_ref, invf_ref, cos_ref, sin_ref):
    ang = pos_ref[...].astype(F32) * invf_ref[...]
    cos_ref[...] = jnp.cos(ang)
    sin_ref[...] = jnp.sin(ang)


def _rope_tables(pos_col, invf):
    s = pos_col.shape[0]
    tm = min(s, 1024)
    return pl.pallas_call(
        _rope_table_kernel,
        grid=(s // tm,),
        in_specs=[pl.BlockSpec((tm, 1), lambda i: (i, 0)),
                  pl.BlockSpec((1, LANE), lambda i: (0, 0))],
        out_specs=[pl.BlockSpec((tm, LANE), lambda i: (i, 0)),
                   pl.BlockSpec((tm, LANE), lambda i: (i, 0))],
        out_shape=[jax.ShapeDtypeStruct((s, LANE), F32)] * 2,
        compiler_params=_cparams("parallel"),
        name="rope_tables",
    )(pos_col, invf)


def _mm_kernel(x_ref, w_ref, o_ref):
    o_ref[...] = jnp.dot(x_ref[...], w_ref[...], preferred_element_type=F32).astype(o_ref.dtype)


def _matmul(x, w, out_dtype, tm, tn, name):
    m, k = x.shape
    n = w.shape[1]
    tm = min(tm, m)
    return pl.pallas_call(
        _mm_kernel,
        grid=(m // tm, n // tn),
        in_specs=[pl.BlockSpec((tm, k), lambda i, j: (i, 0)),
                  pl.BlockSpec((k, tn), lambda i, j: (0, j))],
        out_specs=pl.BlockSpec((tm, tn), lambda i, j: (i, j)),
        out_shape=jax.ShapeDtypeStruct((m, n), out_dtype),
        compiler_params=_cparams("parallel", "parallel"),
        name=name,
    )(x, w)


def _rms(x, g):
    ms = jnp.mean(x * x, axis=-1, keepdims=True)
    return x * lax.rsqrt(ms + RMS_EPS) * g


def _mla_prep_kernel(lat_ref, qg_ref, kvg_ref, wq_ref, wkv_ref, cos_ref, sin_ref, q_ref, k_ref, v_ref):
    nq = MLA_HEADS * MLA_QK_PAD
    cqn = _rms(lat_ref[:, 0:MLA_Q_RANK], qg_ref[...]).astype(BF16)
    ckvn = _rms(lat_ref[:, MLA_Q_RANK:MLA_Q_RANK + MLA_KV_RANK], kvg_ref[...]).astype(BF16)
    qa = jnp.dot(cqn, wq_ref[...], preferred_element_type=F32)
    kv = jnp.dot(ckvn, wkv_ref[...], preferred_element_type=F32)
    cos = cos_ref[...]
    sin = sin_ref[...]
    krot = (lat_ref[:, 768:896] * cos + lat_ref[:, 896:1024] * sin).astype(BF16)
    for h in range(MLA_HEADS):
        c0 = h * MLA_QK_PAD
        q_ref[:, c0:c0 + LANE] = qa[:, c0:c0 + LANE].astype(BF16)
        q_ref[:, c0 + LANE:c0 + 2 * LANE] = (
            qa[:, c0 + LANE:c0 + 2 * LANE] * cos + qa[:, nq + h * LANE:nq + (h + 1) * LANE] * sin).astype(BF16)
        k_ref[:, c0:c0 + LANE] = kv[:, h * LANE:(h + 1) * LANE].astype(BF16)
        k_ref[:, c0 + LANE:c0 + 2 * LANE] = krot
    v_ref[...] = kv[:, MLA_WIDTH:].astype(BF16)


def _mla_prep(h1, qg, kvg, wq, wkv, cos, sin):
    s = h1.shape[0]
    tm = min(s, 512)
    nq = MLA_HEADS * MLA_QK_PAD
    full = lambda shape: pl.BlockSpec(shape, lambda i: (0, 0))
    return pl.pallas_call(
        _mla_prep_kernel,
        grid=(s // tm,),
        in_specs=[pl.BlockSpec((tm, H1_LAT_W), lambda i: (i, 0)),
                  full((1, MLA_Q_RANK)), full((1, MLA_KV_RANK)),
                  full(wq.shape), full(wkv.shape),
                  pl.BlockSpec((tm, LANE), lambda i: (i, 0)),
                  pl.BlockSpec((tm, LANE), lambda i: (i, 0))],
        out_specs=[pl.BlockSpec((tm, nq), lambda i: (i, 0)),
                   pl.BlockSpec((tm, nq), lambda i: (i, 0)),
                   pl.BlockSpec((tm, MLA_WIDTH), lambda i: (i, 0))],
        out_shape=[jax.ShapeDtypeStruct((s, nq), BF16),
                   jax.ShapeDtypeStruct((s, nq), BF16),
                   jax.ShapeDtypeStruct((s, MLA_WIDTH), BF16)],
        compiler_params=_cparams("parallel"),
        name="mla_prep",
    )(h1, qg, kvg, wq, wkv, cos, sin)


def _mla_attn_kernel(q_ref, k_ref, v_ref, g_ref, o_ref, m_sc, l_sc, acc_sc, *, tq):
    qi = pl.program_id(1)
    q = q_ref[...]
    m_sc[...] = jnp.full(m_sc.shape, NEG_BIG, F32)
    l_sc[...] = jnp.zeros(l_sc.shape, F32)
    acc_sc[...] = jnp.zeros(acc_sc.shape, F32)

    def step(j, masked):
        start = pl.multiple_of(j * tq, tq)
        k = k_ref[pl.ds(start, tq), :]
        v = v_ref[pl.ds(start, tq), :]
        s = lax.dot_general(q, k, (((1,), (1,)), ((), ())), preferred_element_type=F32) * MLA_SCALE
        if masked:
            qc = lax.broadcasted_iota(jnp.int32, s.shape, 0) // CHUNK
            kc = lax.broadcasted_iota(jnp.int32, s.shape, 1) // CHUNK
            s = jnp.where(kc <= qc, s, NEG_BIG)
        m_prev = m_sc[...]
        m_new = jnp.maximum(m_prev, jnp.max(s, axis=-1, keepdims=True))
        p = jnp.exp(s - m_new)
        alpha = jnp.exp(m_prev - m_new)
        l_sc[...] = alpha * l_sc[...] + jnp.sum(p, axis=-1, keepdims=True)
        acc_sc[...] = alpha * acc_sc[...] + jnp.dot(p.astype(BF16), v, preferred_element_type=F32)
        m_sc[...] = m_new

    def body(j, carry):
        step(j, False)
        return carry

    lax.fori_loop(0, qi, body, 0)
    step(qi, True)
    o = acc_sc[...] / l_sc[...]
    o_ref[...] = (o * jax.nn.silu(g_ref[...])).astype(o_ref.dtype)


def _mla_attn(q, k, v, h1, tq=512):
    s = q.shape[0]
    tq = min(tq, s)
    ga_blk = H1_GA // LANE
    return pl.pallas_call(
        functools.partial(_mla_attn_kernel, tq=tq),
        grid=(MLA_HEADS, s // tq),
        in_specs=[pl.BlockSpec((tq, MLA_QK_PAD), lambda h, i: (i, h)),
                  pl.BlockSpec((s, MLA_QK_PAD), lambda h, i: (0, h)),
                  pl.BlockSpec((s, MLA_V), lambda h, i: (0, h)),
                  pl.BlockSpec((tq, LANE), lambda h, i: (i, ga_blk + h))],
        out_specs=pl.BlockSpec((tq, MLA_V), lambda h, i: (i, h)),
        out_shape=jax.ShapeDtypeStruct((s, MLA_WIDTH), BF16),
        scratch_shapes=[pltpu.VMEM((tq, 1), F32), pltpu.VMEM((tq, 1), F32), pltpu.VMEM((tq, MLA_V), F32)],
        compiler_params=_cparams("parallel", "parallel"),
        name="mla_attn",
    )(q, k, v, h1)


def _sb_attn_kernel(q_ref, k_ref, v_ref, tri_ref, g_ref, o_ref, r_sc, acc_sc, *, tq):
    qi = pl.program_id(1)
    q = q_ref[...]
    tri = tri_ref[...]
    r_sc[...] = jnp.zeros(r_sc.shape, F32)
    acc_sc[...] = jnp.zeros(acc_sc.shape, F32)

    def step(j, diagonal):
        start = pl.multiple_of(j * tq, tq)
        k = k_ref[pl.ds(start, tq), :]
        v = v_ref[pl.ds(start, tq), :]
        z = lax.dot_general(q, k, (((1,), (1,)), ((), ())), preferred_element_type=F32) * SB_SCALE
        t = jnp.log1p(jnp.exp(-jnp.abs(z)))
        log_beta = jnp.minimum(z, 0.0) - t
        log_1mb = jnp.minimum(-z, 0.0) - t
        if diagonal:
            strict = (lax.broadcasted_iota(jnp.int32, z.shape, 1)
                      < lax.broadcasted_iota(jnp.int32, z.shape, 0))
            log_1mb = jnp.where(strict, log_1mb, 0.0)
        hi = log_1mb.astype(BF16)
        lo = (log_1mb - hi.astype(F32)).astype(BF16)
        rev = (jnp.dot(hi, tri, preferred_element_type=F32)
               + jnp.dot(lo, tri, preferred_element_type=F32))
        r_prev = r_sc[...]
        a = jnp.exp(log_beta + rev + r_prev)
        if diagonal:
            a = jnp.where(strict, a, 0.0)
        acc_sc[...] += jnp.dot(a.astype(BF16), v, preferred_element_type=F32)
        r_sc[...] = r_prev + jnp.sum(log_1mb, axis=-1, keepdims=True)

    step(qi, True)

    def body(i, carry):
        step(qi - 1 - i, False)
        return carry

    lax.fori_loop(0, qi, body, 0)
    o_ref[...] = (acc_sc[...] * jax.nn.silu(g_ref[...])).astype(o_ref.dtype)


def _sb_attn(h2, h1, tri, tq):
    s = h2.shape[0]
    nh = SB_HEADS
    qb, kb, vb, gb = H2_SBQ // LANE, H2_SBK // LANE, H2_SBV // LANE, H1_GC // LANE
    return pl.pallas_call(
        functools.partial(_sb_attn_kernel, tq=tq),
        grid=(nh, s // tq),
        in_specs=[pl.BlockSpec((tq, LANE), lambda h, i: (i, qb + h)),
                  pl.BlockSpec((s, LANE), lambda h, i: (0, kb + h)),
                  pl.BlockSpec((s, LANE), lambda h, i: (0, vb + h)),
                  pl.BlockSpec((tq, tq), lambda h, i: (0, 0)),
                  pl.BlockSpec((tq, LANE), lambda h, i: (i, gb + h))],
        out_specs=pl.BlockSpec((tq, LANE), lambda h, i: (i, h)),
        out_shape=jax.ShapeDtypeStruct((s, SB_WIDTH), BF16),
        scratch_shapes=[pltpu.VMEM((tq, 1), F32), pltpu.VMEM((tq, SB_HEAD_DIM), F32)],
        compiler_params=_cparams("parallel", "parallel"),
        name="sb_attn",
    )(h2, h2, h2, tri, h1)


def _gmlp_kernel(u_ref, v_ref, g_ref, lng_ref, lnb_ref, w_ref, b_ref, o_ref, *, n_chunks):
    gv = jax.nn.gelu(v_ref[...])
    mu = jnp.mean(gv, axis=-1, keepdims=True)
    vc = gv - mu
    var = jnp.mean(vc * vc, axis=-1, keepdims=True)
    vn = (vc * lax.rsqrt(var + LN_EPS) * lng_ref[...] + lnb_ref[...]).astype(BF16)
    t_chunk = lax.broadcasted_iota(jnp.int32, (SG_CHUNK, SG_CHUNK), 0) // CHUNK
    s_chunk = lax.broadcasted_iota(jnp.int32, (SG_CHUNK, SG_CHUNK), 1) // CHUNK
    mask = (s_chunk <= t_chunk).astype(F32)
    for g in range(SG_GROUPS):
        w_sp = (w_ref[g] * mask).astype(BF16)
        bias = b_ref[:, g:g + 1]
        cols = slice(g * SG_GROUP_CH, (g + 1) * SG_GROUP_CH)
        for n in range(n_chunks):
            rows = slice(n * SG_CHUNK, (n + 1) * SG_CHUNK)
            mixed = jnp.dot(w_sp, vn[rows, cols], preferred_element_type=F32) + bias
            o_b = jax.nn.gelu(u_ref[rows, cols]) * mixed
            o_ref[rows, cols] = (o_b * jax.nn.silu(g_ref[rows, cols])).astype(o_ref.dtype)


def _gmlp(h1, lng, lnb, w, b_t):
    s = h1.shape[0]
    tm = min(s, 512)
    full2 = lambda shape: pl.BlockSpec(shape, lambda i: (0, 0))
    return pl.pallas_call(
        functools.partial(_gmlp_kernel, n_chunks=tm // SG_CHUNK),
        grid=(s // tm,),
        in_specs=[pl.BlockSpec((tm, SG_WIDTH), lambda i: (i, H1_SGU // SG_WIDTH)),
                  pl.BlockSpec((tm, SG_WIDTH), lambda i: (i, H1_SGV // SG_WIDTH)),
                  pl.BlockSpec((tm, SG_WIDTH), lambda i: (i, H1_GB // SG_WIDTH)),
                  full2((1, SG_WIDTH)), full2((1, SG_WIDTH)),
                  pl.BlockSpec((SG_GROUPS, SG_CHUNK, SG_CHUNK), lambda i: (0, 0, 0)),
                  full2((SG_CHUNK, SG_GROUPS))],
        out_specs=pl.BlockSpec((tm, SG_WIDTH), lambda i: (i, 0)),
        out_shape=jax.ShapeDtypeStruct((s, SG_WIDTH), BF16),
        compiler_params=_cparams("parallel"),
        name="gmlp",
    )(h1, h1, h1, lng, lnb, w, b_t)


def _mem_kv_kernel(mem_ref, wk_ref, wv_ref, kbd_ref, vbd_ref):
    mem = mem_ref[...].astype(BF16)
    mk_t = jnp.dot(mem, wk_ref[...], preferred_element_type=F32).T
    mv = jnp.dot(mem, wv_ref[...], preferred_element_type=F32)
    feat_row = lax.broadcasted_iota(jnp.int32, mk_t.shape, 0) // MEM_HEAD_DIM
    feat_col = lax.broadcasted_iota(jnp.int32, mv.shape, 1) // MEM_HEAD_DIM
    for h in range(MEM_HEADS):
        seg = slice(h * MEM_TOKENS, (h + 1) * MEM_TOKENS)
        kbd_ref[:, seg] = jnp.where(feat_row == h, mk_t, 0.0).astype(BF16)
        vbd_ref[seg, :] = jnp.where(feat_col == h, mv, 0.0).astype(BF16)


def _mem_kv(mem, wk, wv):
    nt = MEM_HEADS * MEM_TOKENS
    full2 = lambda shape: pl.BlockSpec(shape, lambda i: (0, 0))
    return pl.pallas_call(
        _mem_kv_kernel,
        grid=(1,),
        in_specs=[full2(mem.shape), full2(wk.shape), full2(wv.shape)],
        out_specs=[full2((MEM_WIDTH, nt)), full2((nt, MEM_WIDTH))],
        out_shape=[jax.ShapeDtypeStruct((MEM_WIDTH, nt), BF16),
                   jax.ShapeDtypeStruct((nt, MEM_WIDTH), BF16)],
        compiler_params=_cparams("arbitrary"),
        name="mem_kv",
    )(mem, wk, wv)


def _mem_attn_kernel(q_ref, kbd_ref, vbd_ref, g_ref, o_ref):
    logits = jnp.dot(q_ref[...], kbd_ref[...], preferred_element_type=F32) * MEM_SCALE
    probs = []
    for h in range(MEM_HEADS):
        seg = logits[:, h * MEM_TOKENS:(h + 1) * MEM_TOKENS]
        e = jnp.exp(seg - jnp.max(seg, axis=-1, keepdims=True))
        probs.append((e / jnp.sum(e, axis=-1, keepdims=True)).astype(BF16))
    p = jnp.concatenate(probs, axis=-1)
    o = jnp.dot(p, vbd_ref[...], preferred_element_type=F32)
    o_ref[...] = (o * jax.nn.silu(g_ref[...])).astype(o_ref.dtype)


def _mem_attn(h2, kbd, vbd, h1):
    s = h2.shape[0]
    tm = min(s, 512)
    full2 = lambda shape: pl.BlockSpec(shape, lambda i: (0, 0))
    return pl.pallas_call(
        _mem_attn_kernel,
        grid=(s // tm,),
        in_specs=[pl.BlockSpec((tm, MEM_WIDTH), lambda i: (i, H2_MQ // MEM_WIDTH)),
                  full2(kbd.shape), full2(vbd.shape),
                  pl.BlockSpec((tm, MEM_WIDTH), lambda i: (i, H1_GM // MEM_WIDTH))],
        out_specs=pl.BlockSpec((tm, MEM_WIDTH), lambda i: (i, 0)),
        out_shape=jax.ShapeDtypeStruct((s, MEM_WIDTH), BF16),
        compiler_params=_cparams("parallel"),
        name="mem_attn",
    )(h2, kbd, vbd, h1)


def _out_ln_kernel(ya_ref, yb_ref, yc_ref, ym_ref, wa_ref, wb_ref, wc_ref, wm_ref, x_ref, g_ref, b_ref,
                   o_ref, obf_ref, *, alpha):
    y = jnp.dot(ya_ref[...], wa_ref[...], preferred_element_type=F32)
    y += jnp.dot(yb_ref[...], wb_ref[...], preferred_element_type=F32)
    y += jnp.dot(yc_ref[...], wc_ref[...], preferred_element_type=F32)
    y += jnp.dot(ym_ref[...], wm_ref[...], preferred_element_type=F32)
    r = alpha * x_ref[...] + y
    mu = jnp.mean(r, axis=-1, keepdims=True)
    rc = r - mu
    var = jnp.mean(rc * rc, axis=-1, keepdims=True)
    out = rc * lax.rsqrt(var + LN_EPS) * g_ref[...] + b_ref[...]
    o_ref[...] = out
    obf_ref[...] = out.astype(BF16)


def _out_ln(ya, yb, yc, ym, wa, wb, wc, wm, x, g, b, alpha):
    s, d = x.shape
    tm = min(s, 256)
    row = lambda w: pl.BlockSpec((tm, w), lambda i: (i, 0))
    full2 = lambda shape: pl.BlockSpec(shape, lambda i: (0, 0))
    return pl.pallas_call(
        functools.partial(_out_ln_kernel, alpha=alpha),
        grid=(s // tm,),
        in_specs=[row(MLA_WIDTH), row(SG_WIDTH), row(SB_WIDTH), row(MEM_WIDTH),
                  full2(wa.shape), full2(wb.shape), full2(wc.shape), full2(wm.shape),
                  row(d), full2((1, d)), full2((1, d))],
        out_specs=[row(d), row(d)],
        out_shape=[jax.ShapeDtypeStruct((s, d), F32), jax.ShapeDtypeStruct((s, d), BF16)],
        compiler_params=_cparams("parallel"),
        name="out_ln",
    )(ya, yb, yc, ym, wa, wb, wc, wm, x, g, b)


def _rot_half_cols(w):
    half = w.shape[-1] // 2
    return jnp.concatenate([-w[..., half:], w[..., :half]], axis=-1)


def _prep_weights(w_in, w_uq, w_ukv, w_mem_k, w_mem_v, w_out):
    depth = w_in.shape[0]
    o = [0]
    for wdt in (MLA_Q_RANK, MLA_KV_RANK, MLA_ROPE, MLA_WIDTH, SG_WIDTH, SG_WIDTH, SG_WIDTH,
                SB_WIDTH, SB_WIDTH, SB_WIDTH, SB_WIDTH, MEM_WIDTH, MEM_WIDTH):
        o.append(o[-1] + wdt)
    col = lambda i: w_in[:, :, o[i]:o[i + 1]]
    c_q, c_kv, k_pe, g_a, sg_u, sg_v, g_b, sb_q, sb_k, sb_v, g_c, m_q, g_m = [col(i) for i in range(13)]
    z64 = jnp.zeros(k_pe.shape, w_in.dtype)
    w1 = jnp.concatenate([c_q, c_kv, k_pe, z64, _rot_half_cols(k_pe), z64,
                          sg_u, sg_v, g_b, g_a, g_c, g_m], axis=-1).astype(BF16)
    w2 = jnp.concatenate([sb_q, sb_k, sb_v, m_q], axis=-1).astype(BF16)

    wq = w_uq.reshape(depth, MLA_Q_RANK, MLA_HEADS, MLA_NOPE + MLA_ROPE)
    q_nope, q_rope = wq[..., :MLA_NOPE], wq[..., MLA_NOPE:]
    zq = jnp.zeros(q_rope.shape, w_uq.dtype)
    wq_main = jnp.concatenate([q_nope, q_rope, zq], axis=-1).reshape(depth, MLA_Q_RANK, MLA_HEADS * MLA_QK_PAD)
    wq_rot = jnp.concatenate([_rot_half_cols(q_rope), zq], axis=-1).reshape(depth, MLA_Q_RANK, MLA_HEADS * LANE)
    wq_all = jnp.concatenate([wq_main, wq_rot], axis=-1).astype(BF16)

    wkv = w_ukv.reshape(depth, MLA_KV_RANK, MLA_HEADS, MLA_NOPE + MLA_V)
    wkv_all = jnp.concatenate([wkv[..., :MLA_NOPE].reshape(depth, MLA_KV_RANK, MLA_WIDTH),
                               wkv[..., MLA_NOPE:].reshape(depth, MLA_KV_RANK, MLA_WIDTH)], axis=-1).astype(BF16)

    wo = w_out.astype(BF16)
    e0, e1, e2 = MLA_WIDTH, MLA_WIDTH + SG_WIDTH, MLA_WIDTH + SG_WIDTH + SB_WIDTH
    return dict(w1=w1, w2=w2, wq=wq_all, wkv=wkv_all,
                wmk=w_mem_k.astype(BF16), wmv=w_mem_v.astype(BF16),
                wo_a=wo[:, :e0], wo_b=wo[:, e0:e1], wo_c=wo[:, e1:e2], wo_m=wo[:, e2:])


def kernel(x, mem, positions, w_in, q_norm_g, w_uq, kv_norm_g, w_ukv, sg_ln_g, sg_ln_b, sg_w, sg_b,
           w_mem_k, w_mem_v, w_out, ln_g, ln_b):
    b, s, d = x.shape
    depth = w_in.shape[0]
    alpha = (2.0 * depth) ** 0.25
    sb_tq = min(s, 256)

    inv_freq = ROPE_THETA ** (-jnp.arange(0, MLA_ROPE, 2, dtype=F32) / MLA_ROPE)
    invf = jnp.concatenate([inv_freq, inv_freq, jnp.zeros((LANE - MLA_ROPE,), F32)])[None, :]
    tri = (lax.broadcasted_iota(jnp.int32, (sb_tq, sb_tq), 0)
           > lax.broadcasted_iota(jnp.int32, (sb_tq, sb_tq), 1)).astype(BF16)
    wts = _prep_weights(w_in, w_uq, w_ukv, w_mem_k, w_mem_v, w_out)
    sg_b_t = jnp.swapaxes(sg_b, 1, 2)

    outs = []
    for bi in range(b):
        xf = x[bi]
        xb = xf.astype(BF16)
        cos, sin = _rope_tables(positions[bi].reshape(s, 1), invf)
        for l in range(depth):
            h1 = _matmul(xb, wts["w1"][l], F32, 1024, 512, "in_proj_f32")
            h2 = _matmul(xb, wts["w2"][l], BF16, 1024, H2_W // 2, "in_proj_bf16")
            q, k, v = _mla_prep(h1, q_norm_g[l][None, :], kv_norm_g[l][None, :], wts["wq"][l], wts["wkv"][l],
                                cos, sin)
            ya = _mla_attn(q, k, v, h1)
            yc = _sb_attn(h2, h1, tri, sb_tq)
            yb = _gmlp(h1, sg_ln_g[l][None, :], sg_ln_b[l][None, :], sg_w[l], sg_b_t[l])
            kbd, vbd = _mem_kv(mem[bi], wts["wmk"][l], wts["wmv"][l])
            ym = _mem_attn(h2, kbd, vbd, h1)
            xf, xb = _out_ln(ya, yb, yc, ym, wts["wo_a"][l], wts["wo_b"][l], wts["wo_c"][l], wts["wo_m"][l],
                             xf, ln_g[l][None, :], ln_b[l][None, :], alpha)
        outs.append(xf)
    return outs[0][None] if b == 1 else jnp.stack(outs, axis=0)
```

```python
import functools
import math

import jax
import jax.numpy as jnp
from jax import lax
from jax.experimental import pallas as pl
from jax.experimental.pallas import tpu as pltpu

F32 = jnp.float32
BF16 = jnp.bfloat16

D_MODEL = 2048
CHUNK = 64
MLA_HEADS = 6
MLA_NOPE = 128
MLA_ROPE = 64
MLA_V = 128
MLA_Q_RANK = 512
MLA_KV_RANK = 256
MLA_WIDTH = MLA_HEADS * MLA_V
ROPE_THETA = 10000.0
SG_GROUPS = 4
SG_GROUP_CH = 128
SG_WIDTH = SG_GROUPS * SG_GROUP_CH
SG_CHUNK = 128
SB_HEADS = 4
SB_HEAD_DIM = 128
SB_WIDTH = SB_HEADS * SB_HEAD_DIM
MEM_TOKENS = 256
MEM_HEADS = 4
MEM_HEAD_DIM = 64
MEM_WIDTH = MEM_HEADS * MEM_HEAD_DIM
LN_EPS = 1e-5
RMS_EPS = 1e-6

LOG2E = math.log2(math.e)
MLA_SCALE = 1.0 / math.sqrt(MLA_NOPE + MLA_ROPE)
SB_SCALE = 1.0 / math.sqrt(SB_HEAD_DIM)
MEM_SCALE = 1.0 / math.sqrt(MEM_HEAD_DIM)

LANE = 128
MXU_TILE = 256
MLA_QK_PAD = 256
NEG_BIG = -1e30
VMEM_LIMIT = 48 * 1024 * 1024
ATT_TQ = 512
ATT_TC = MXU_TILE

H1_LAT_W = 1024
H1_SGU = 1024
H1_SGV = 1536
H1_GB = 2048
H1_GA = 2560
H1_GC = 3328
H1_GM = 3840
H1_W = 4096
H2_SBK = 0
H2_MQ = 512
H2_W = 768
H2T_SBQ = 0
H2T_SBV = 512
H2T_W = 1024

_NT = (((1,), (1,)), ((), ()))


def _cparams(*sem, flags=None):
    return pltpu.CompilerParams(dimension_semantics=sem, vmem_limit_bytes=VMEM_LIMIT, flags=flags)


ATT_FLAGS = None


def _rope_table_kernel(pos_col_ref, pos_row_ref, invf_row_ref, invf_col_ref, cos_ref, sin_ref, cos_t_ref, sin_t_ref):
    ang = pos_col_ref[...].astype(F32) * invf_row_ref[...]
    cos_ref[...] = jnp.cos(ang)
    sin_ref[...] = jnp.sin(ang)
    ang_t = invf_col_ref[...] * pos_row_ref[...].astype(F32)
    cos_t_ref[...] = jnp.cos(ang_t)
    sin_t_ref[...] = jnp.sin(ang_t)


def _rope_tables(pos_col, pos_row, invf_row, invf_col):
    s = pos_col.shape[0]
    tm = min(s, 1024)
    return pl.pallas_call(
        _rope_table_kernel,
        grid=(s // tm,),
        in_specs=[pl.BlockSpec((tm, 1), lambda i: (i, 0)),
                  pl.BlockSpec((1, tm), lambda i: (0, i)),
                  pl.BlockSpec((1, LANE), lambda i: (0, 0)),
                  pl.BlockSpec((LANE, 1), lambda i: (0, 0))],
        out_specs=[pl.BlockSpec((tm, LANE), lambda i: (i, 0)),
                   pl.BlockSpec((tm, LANE), lambda i: (i, 0)),
                   pl.BlockSpec((LANE, tm), lambda i: (0, i)),
                   pl.BlockSpec((LANE, tm), lambda i: (0, i))],
        out_shape=[jax.ShapeDtypeStruct((s, LANE), F32)] * 2 + [jax.ShapeDtypeStruct((LANE, s), F32)] * 2,
        compiler_params=_cparams("parallel"),
        name="rope_tables",
    )(pos_col, pos_row, invf_row, invf_col)


def _mm_kernel(x_ref, w_ref, o_ref):
    o_ref[...] = jnp.dot(x_ref[...], w_ref[...], preferred_element_type=F32).astype(o_ref.dtype)


def _matmul(x, w, out_dtype, tm, tn, name):
    m, k = x.shape
    n = w.shape[1]
    tm = min(tm, m)
    return pl.pallas_call(
        _mm_kernel,
        grid=(m // tm, n // tn),
        in_specs=[pl.BlockSpec((tm, k), lambda i, j: (i, 0)),
                  pl.BlockSpec((k, tn), lambda i, j: (0, j))],
        out_specs=pl.BlockSpec((tm, tn), lambda i, j: (i, j)),
        out_shape=jax.ShapeDtypeStruct((m, n), out_dtype),
        compiler_params=_cparams("parallel", "parallel"),
        name=name,
    )(x, w)


def _mm_nt_kernel(wt_ref, x_ref, o_ref):
    o_ref[...] = lax.dot_general(wt_ref[...], x_ref[...], _NT, preferred_element_type=F32).astype(o_ref.dtype)


def _matmul_nt(wt, x, out_dtype, tm, tn, name):
    n, k = wt.shape
    m = x.shape[0]
    tm = min(tm, m)
    return pl.pallas_call(
        _mm_nt_kernel,
        grid=(m // tm, n // tn),
        in_specs=[pl.BlockSpec((tn, k), lambda i, j: (j, 0)),
                  pl.BlockSpec((tm, k), lambda i, j: (i, 0))],
        out_specs=pl.BlockSpec((tn, tm), lambda i, j: (j, i)),
        out_shape=jax.ShapeDtypeStruct((n, m), out_dtype),
        compiler_params=_cparams("parallel", "parallel"),
        name=name,
    )(wt, x)


def _rms(x, g):
    ms = jnp.mean(x * x, axis=-1, keepdims=True)
    return x * lax.rsqrt(ms + RMS_EPS) * g


def _mla_prep_kernel(lat_ref, qg_ref, kvg_ref, wqt_ref, wk_ref, wvt_ref, cos_ref, sin_ref, cos_t_ref, sin_t_ref,
                     qt_ref, k_ref, vt_ref):
    nq = MLA_HEADS * MLA_QK_PAD
    cqn = _rms(lat_ref[:, 0:MLA_Q_RANK], qg_ref[...]).astype(BF16)
    ckvn = _rms(lat_ref[:, MLA_Q_RANK:MLA_Q_RANK + MLA_KV_RANK], kvg_ref[...]).astype(BF16)
    qa_t = lax.dot_general(wqt_ref[...], cqn, _NT, preferred_element_type=F32)
    kn = jnp.dot(ckvn, wk_ref[...], preferred_element_type=F32)
    vt_ref[...] = lax.dot_general(wvt_ref[...], ckvn, _NT, preferred_element_type=F32).astype(BF16)
    cos_t = cos_t_ref[...]
    sin_t = sin_t_ref[...]
    krot = (lat_ref[:, 768:896] * cos_ref[...] + lat_ref[:, 896:1024] * sin_ref[...]).astype(BF16)
    for h in range(MLA_HEADS):
        c0 = h * MLA_QK_PAD
        qt_ref[c0:c0 + LANE, :] = qa_t[c0:c0 + LANE, :].astype(BF16)
        qt_ref[c0 + LANE:c0 + 2 * LANE, :] = (
            qa_t[c0 + LANE:c0 + 2 * LANE, :] * cos_t
            + qa_t[nq + h * LANE:nq + (h + 1) * LANE, :] * sin_t).astype(BF16)
        k_ref[:, c0:c0 + LANE] = kn[:, h * LANE:(h + 1) * LANE].astype(BF16)
        k_ref[:, c0 + LANE:c0 + 2 * LANE] = krot


def _mla_prep(h1, qg, kvg, wqt, wk, wvt, cos, sin, cos_t, sin_t):
    s = h1.shape[0]
    tm = min(s, 512)
    nq = MLA_HEADS * MLA_QK_PAD
    full = lambda shape: pl.BlockSpec(shape, lambda i: (0, 0))
    rows = lambda w: pl.BlockSpec((tm, w), lambda i: (i, 0))
    cols = lambda w: pl.BlockSpec((w, tm), lambda i: (0, i))
    return pl.pallas_call(
        _mla_prep_kernel,
        grid=(s // tm,),
        in_specs=[rows(H1_LAT_W), full((1, MLA_Q_RANK)), full((1, MLA_KV_RANK)),
                  full(wqt.shape), full(wk.shape), full(wvt.shape),
                  rows(LANE), rows(LANE), cols(LANE), cols(LANE)],
        out_specs=[cols(nq), rows(nq), cols(MLA_WIDTH)],
        out_shape=[jax.ShapeDtypeStruct((nq, s), BF16),
                   jax.ShapeDtypeStruct((s, nq), BF16),
                   jax.ShapeDtypeStruct((MLA_WIDTH, s), BF16)],
        compiler_params=_cparams("parallel"),
        name="mla_prep",
    )(h1, qg, kvg, wqt, wk, wvt, cos, sin, cos_t, sin_t)


def _mla_attn_kernel(qt_ref, k_ref, vt_ref, g_ref, o_ref, m_sc, l_sc, acc_sc, s_buf, p_buf, al_buf, *, tq, tc):
    n = pl.program_id(1) + 1
    m_sc[...] = jnp.full(m_sc.shape, NEG_BIG, F32)
    l_sc[...] = jnp.zeros(l_sc.shape, F32)
    acc_sc[...] = jnp.zeros(acc_sc.shape, F32)
    col_tiles = [slice(c * tc, (c + 1) * tc) for c in range(tq // tc)]

    def stage_scores(j, slot):
        k = k_ref[pl.ds(pl.multiple_of(j * tq, tq), tq), :]
        for cols in col_tiles:
            s_buf[slot, :, cols] = jnp.dot(k, qt_ref[:, cols], preferred_element_type=F32) * (MLA_SCALE * LOG2E)

    def stage_softmax(slot, masked):
        for c, cols in enumerate(col_tiles):
            first_qc = c * tc // CHUNK
            last_qc = ((c + 1) * tc - 1) // CHUNK
            qc = lax.broadcasted_iota(jnp.int32, (CHUNK, tc), 1) // CHUNK + first_qc

            def load(g):
                x = s_buf[slot, g * CHUNK:(g + 1) * CHUNK, cols]
                if masked and g > first_qc:
                    x = jnp.where(qc >= g, x, NEG_BIG)
                return x

            groups = [g for g in range(tq // CHUNK) if not (masked and g > last_qc)]
            mx = load(groups[0])
            for g in groups[1:]:
                mx = jnp.maximum(mx, load(g))
            m_prev = m_sc[:, cols]
            m_new = jnp.maximum(m_prev, jnp.max(mx, axis=0, keepdims=True))
            alpha = jnp.exp2(m_prev - m_new)
            psum = None
            for g in range(tq // CHUNK):
                rows = slice(g * CHUNK, (g + 1) * CHUNK)
                if g in groups:
                    p = jnp.exp2(load(g) - m_new)
                    psum = p if psum is None else psum + p
                    p_buf[slot, rows, cols] = p.astype(BF16)
                else:
                    p_buf[slot, rows, cols] = jnp.zeros((CHUNK, tc), BF16)
            l_sc[:, cols] = alpha * l_sc[:, cols] + jnp.sum(psum, axis=0, keepdims=True)
            m_sc[:, cols] = m_new
            al_buf[slot, :, cols] = alpha

    def stage_values(j, slot):
        vt = vt_ref[:, pl.ds(pl.multiple_of(j * tq, tq), tq)]
        for cols in col_tiles:
            pv = jnp.dot(vt, p_buf[slot, :, cols], preferred_element_type=F32)
            acc_sc[:, cols] = al_buf[slot, :, cols] * acc_sc[:, cols] + pv

    def trip(i, slot):
        stage_scores(i, slot)
        stage_values(i - 2, slot)
        stage_softmax(1 - slot, False)

    def drain(last_slot):
        stage_values(n - 2, 1 - last_slot)
        stage_softmax(last_slot, True)
        stage_values(n - 1, last_slot)

    stage_scores(0, 0)

    @pl.when(n >= 2)
    def _():
        stage_softmax(0, False)
        stage_scores(1, 1)

    def body(ii, carry):
        i = 2 + 2 * ii
        trip(i, 0)
        trip(i + 1, 1)
        return carry

    lax.fori_loop(0, (n - 2) // 2, body, 0)
    n_odd = n % 2 == 1

    @pl.when(jnp.logical_and(n_odd, n >= 3))
    def _():
        trip(n - 1, 0)
        drain(0)

    @pl.when(jnp.logical_not(n_odd))
    def _():
        drain(1)

    @pl.when(n == 1)
    def _():
        stage_softmax(0, True)
        stage_values(0, 0)

    o = (acc_sc[...] / l_sc[...]).T
    o_ref[...] = (o * jax.nn.silu(g_ref[...])).astype(o_ref.dtype)


def _mla_attn(qt, k, vt, h1):
    s = k.shape[0]
    tq = min(ATT_TQ, s)
    tc = min(ATT_TC, tq)
    ga_blk = H1_GA // LANE
    return pl.pallas_call(
        functools.partial(_mla_attn_kernel, tq=tq, tc=tc),
        grid=(MLA_HEADS, s // tq),
        in_specs=[pl.BlockSpec((MLA_QK_PAD, tq), lambda h, i: (h, i)),
                  pl.BlockSpec((s, MLA_QK_PAD), lambda h, i: (0, h)),
                  pl.BlockSpec((MLA_V, s), lambda h, i: (h, 0)),
                  pl.BlockSpec((tq, LANE), lambda h, i: (i, ga_blk + h))],
        out_specs=pl.BlockSpec((tq, MLA_V), lambda h, i: (i, h)),
        out_shape=jax.ShapeDtypeStruct((s, MLA_WIDTH), BF16),
        scratch_shapes=[pltpu.VMEM((1, tq), F32), pltpu.VMEM((1, tq), F32), pltpu.VMEM((MLA_V, tq), F32),
                        pltpu.VMEM((2, tq, tq), F32), pltpu.VMEM((2, tq, tq), BF16), pltpu.VMEM((2, 1, tq), F32)],
        compiler_params=_cparams("parallel", "parallel", flags=ATT_FLAGS),
        name="mla_attn",
    )(qt, k, vt, h1)


def _sb_attn_kernel(qt_ref, k_ref, vt_ref, tri_ref, g_ref, o_ref, r_sc, acc_sc, *, tq, tc):
    qi = pl.program_id(1)
    r_sc[...] = jnp.zeros(r_sc.shape, F32)
    acc_sc[...] = jnp.zeros(acc_sc.shape, F32)
    n_sub = tq // tc

    def tile(start, c, diagonal):
        cols = slice(c * tc, (c + 1) * tc)
        k = k_ref[pl.ds(start, tc), :]
        z = jnp.dot(k, qt_ref[:, cols], preferred_element_type=F32) * SB_SCALE
        t = jnp.log1p(jnp.exp(-jnp.abs(z)))
        neg_part = jnp.minimum(z, 0.0)
        log_beta = neg_part - t
        log_1mb = (neg_part - z) - t
        if diagonal:
            strict = (lax.broadcasted_iota(jnp.int32, z.shape, 0)
                      < lax.broadcasted_iota(jnp.int32, z.shape, 1))
            log_1mb = jnp.where(strict, log_1mb, 0.0)
        hi = log_1mb.astype(BF16)
        lo = (log_1mb - hi.astype(F32)).astype(BF16)
        tri = tri_ref[...]
        rev = (jnp.dot(tri, hi, preferred_element_type=F32)
               + jnp.dot(tri, lo, preferred_element_type=F32))
        r_prev = r_sc[:, cols]
        a = jnp.exp(log_beta + rev + r_prev)
        if diagonal:
            a = jnp.where(strict, a, 0.0)
        acc_sc[:, cols] += jnp.dot(vt_ref[:, pl.ds(start, tc)], a.astype(BF16), preferred_element_type=F32)
        r_sc[:, cols] = r_prev + jnp.sum(log_1mb, axis=0, keepdims=True)

    base = qi * tq
    for kk in range(n_sub - 1, -1, -1):
        for c in range(kk, n_sub):
            tile(pl.multiple_of(base + kk * tc, tc), c, kk == c)

    def body(i, carry):
        start = (qi - 1 - i) * tq
        for kk in range(n_sub - 1, -1, -1):
            for c in range(n_sub):
                tile(pl.multiple_of(start + kk * tc, tc), c, False)
        return carry

    lax.fori_loop(0, qi, body, 0)
    o_ref[...] = (acc_sc[...].T * jax.nn.silu(g_ref[...])).astype(o_ref.dtype)


def _sb_attn(h2, h2t, h1, tri):
    s = h2.shape[0]
    tq = min(ATT_TQ, s)
    tc = tri.shape[0]
    qb, kb, vb, gb = H2T_SBQ // LANE, H2_SBK // LANE, H2T_SBV // LANE, H1_GC // LANE
    return pl.pallas_call(
        functools.partial(_sb_attn_kernel, tq=tq, tc=tc),
        grid=(SB_HEADS, s // tq),
        in_specs=[pl.BlockSpec((LANE, tq), lambda h, i: (qb + h, i)),
                  pl.BlockSpec((s, LANE), lambda h, i: (0, kb + h)),
                  pl.BlockSpec((LANE, s), lambda h, i: (vb + h, 0)),
                  pl.BlockSpec((tc, tc), lambda h, i: (0, 0)),
                  pl.BlockSpec((tq, LANE), lambda h, i: (i, gb + h))],
        out_specs=pl.BlockSpec((tq, LANE), lambda h, i: (i, h)),
        out_shape=jax.ShapeDtypeStruct((s, SB_WIDTH), BF16),
        scratch_shapes=[pltpu.VMEM((1, tq), F32), pltpu.VMEM((SB_HEAD_DIM, tq), F32)],
        compiler_params=_cparams("parallel", "parallel", flags=ATT_FLAGS),
        name="sb_attn",
    )(h2t, h2, h2t, tri, h1)


def _gmlp_kernel(u_ref, v_ref, g_ref, lng_ref, lnb_ref, w_ref, b_ref, o_ref, *, n_chunks):
    gv = jax.nn.gelu(v_ref[...])
    mu = jnp.mean(gv, axis=-1, keepdims=True)
    vc = gv - mu
    var = jnp.mean(vc * vc, axis=-1, keepdims=True)
    vn = (vc * lax.rsqrt(var + LN_EPS) * lng_ref[...] + lnb_ref[...]).astype(BF16)
    t_chunk = lax.broadcasted_iota(jnp.int32, (SG_CHUNK, SG_CHUNK), 0) // CHUNK
    s_chunk = lax.broadcasted_iota(jnp.int32, (SG_CHUNK, SG_CHUNK), 1) // CHUNK
    mask = (s_chunk <= t_chunk).astype(F32)
    for g in range(SG_GROUPS):
        w_sp = (w_ref[g] * mask).astype(BF16)
        bias = b_ref[:, g:g + 1]
        cols = slice(g * SG_GROUP_CH, (g + 1) * SG_GROUP_CH)
        for n in range(n_chunks):
            rows = slice(n * SG_CHUNK, (n + 1) * SG_CHUNK)
            mixed = jnp.dot(w_sp, vn[rows, cols], preferred_element_type=F32) + bias
            o_b = jax.nn.gelu(u_ref[rows, cols]) * mixed
            o_ref[rows, cols] = (o_b * jax.nn.silu(g_ref[rows, cols])).astype(o_ref.dtype)


def _gmlp(h1, lng, lnb, w, b_t):
    s = h1.shape[0]
    tm = min(s, 512)
    full2 = lambda shape: pl.BlockSpec(shape, lambda i: (0, 0))
    return pl.pallas_call(
        functools.partial(_gmlp_kernel, n_chunks=tm // SG_CHUNK),
        grid=(s // tm,),
        in_specs=[pl.BlockSpec((tm, SG_WIDTH), lambda i: (i, H1_SGU // SG_WIDTH)),
                  pl.BlockSpec((tm, SG_WIDTH), lambda i: (i, H1_SGV // SG_WIDTH)),
                  pl.BlockSpec((tm, SG_WIDTH), lambda i: (i, H1_GB // SG_WIDTH)),
                  full2((1, SG_WIDTH)), full2((1, SG_WIDTH)),
                  pl.BlockSpec((SG_GROUPS, SG_CHUNK, SG_CHUNK), lambda i: (0, 0, 0)),
                  full2((SG_CHUNK, SG_GROUPS))],
        out_specs=pl.BlockSpec((tm, SG_WIDTH), lambda i: (i, 0)),
        out_shape=jax.ShapeDtypeStruct((s, SG_WIDTH), BF16),
        compiler_params=_cparams("parallel"),
        name="gmlp",
    )(h1, h1, h1, lng, lnb, w, b_t)


def _mem_kv_kernel(mem_ref, wk_ref, wv_ref, kbd_ref, vbd_ref):
    mem = mem_ref[...].astype(BF16)
    mk_t = jnp.dot(mem, wk_ref[...], preferred_element_type=F32).T
    mv = jnp.dot(mem, wv_ref[...], preferred_element_type=F32)
    feat_row = lax.broadcasted_iota(jnp.int32, mk_t.shape, 0) // MEM_HEAD_DIM
    feat_col = lax.broadcasted_iota(jnp.int32, mv.shape, 1) // MEM_HEAD_DIM
    for h in range(MEM_HEADS):
        seg = slice(h * MEM_TOKENS, (h + 1) * MEM_TOKENS)
        kbd_ref[:, seg] = jnp.where(feat_row == h, mk_t, 0.0).astype(BF16)
        vbd_ref[seg, :] = jnp.where(feat_col == h, mv, 0.0).astype(BF16)


def _mem_kv(mem, wk, wv):
    nt = MEM_HEADS * MEM_TOKENS
    full2 = lambda shape: pl.BlockSpec(shape, lambda i: (0, 0))
    return pl.pallas_call(
        _mem_kv_kernel,
        grid=(1,),
        in_specs=[full2(mem.shape), full2(wk.shape), full2(wv.shape)],
        out_specs=[full2((MEM_WIDTH, nt)), full2((nt, MEM_WIDTH))],
        out_shape=[jax.ShapeDtypeStruct((MEM_WIDTH, nt), BF16),
                   jax.ShapeDtypeStruct((nt, MEM_WIDTH), BF16)],
        compiler_params=_cparams("arbitrary"),
        name="mem_kv",
    )(mem, wk, wv)


def _mem_attn_kernel(q_ref, kbd_ref, vbd_ref, g_ref, o_ref):
    logits = jnp.dot(q_ref[...], kbd_ref[...], preferred_element_type=F32) * MEM_SCALE
    probs = []
    for h in range(MEM_HEADS):
        seg = logits[:, h * MEM_TOKENS:(h + 1) * MEM_TOKENS]
        e = jnp.exp(seg - jnp.max(seg, axis=-1, keepdims=True))
        probs.append((e / jnp.sum(e, axis=-1, keepdims=True)).astype(BF16))
    p = jnp.concatenate(probs, axis=-1)
    o = jnp.dot(p, vbd_ref[...], preferred_element_type=F32)
    o_ref[...] = (o * jax.nn.silu(g_ref[...])).astype(o_ref.dtype)


def _mem_attn(h2, kbd, vbd, h1):
    s = h2.shape[0]
    tm = min(s, 512)
    full2 = lambda shape: pl.BlockSpec(shape, lambda i: (0, 0))
    return pl.pallas_call(
        _mem_attn_kernel,
        grid=(s // tm,),
        in_specs=[pl.BlockSpec((tm, MEM_WIDTH), lambda i: (i, H2_MQ // MEM_WIDTH)),
                  full2(kbd.shape), full2(vbd.shape),
                  pl.BlockSpec((tm, MEM_WIDTH), lambda i: (i, H1_GM // MEM_WIDTH))],
        out_specs=pl.BlockSpec((tm, MEM_WIDTH), lambda i: (i, 0)),
        out_shape=jax.ShapeDtypeStruct((s, MEM_WIDTH), BF16),
        compiler_params=_cparams("parallel"),
        name="mem_attn",
    )(h2, kbd, vbd, h1)


def _out_ln_kernel(ya_ref, yb_ref, yc_ref, ym_ref, wa_ref, wb_ref, wc_ref, wm_ref, x_ref, g_ref, b_ref,
                   o_ref, obf_ref, *, alpha):
    y = jnp.dot(ya_ref[...], wa_ref[...], preferred_element_type=F32)
    y += jnp.dot(yb_ref[...], wb_ref[...], preferred_element_type=F32)
    y += jnp.dot(yc_ref[...], wc_ref[...], preferred_element_type=F32)
    y += jnp.dot(ym_ref[...], wm_ref[...], preferred_element_type=F32)
    r = alpha * x_ref[...] + y
    mu = jnp.mean(r, axis=-1, keepdims=True)
    rc = r - mu
    var = jnp.mean(rc * rc, axis=-1, keepdims=True)
    out = rc * lax.rsqrt(var + LN_EPS) * g_ref[...] + b_ref[...]
    o_ref[...] = out
    obf_ref[...] = out.astype(BF16)


def _out_ln(ya, yb, yc, ym, wa, wb, wc, wm, x, g, b, alpha):
    s, d = x.shape
    tm = min(s, 256)
    row = lambda w: pl.BlockSpec((tm, w), lambda i: (i, 0))
    full2 = lambda shape: pl.BlockSpec(shape, lambda i: (0, 0))
    return pl.pallas_call(
        functools.partial(_out_ln_kernel, alpha=alpha),
        grid=(s // tm,),
        in_specs=[row(MLA_WIDTH), row(SG_WIDTH), row(SB_WIDTH), row(MEM_WIDTH),
                  full2(wa.shape), full2(wb.shape), full2(wc.shape), full2(wm.shape),
                  row(d), full2((1, d)), full2((1, d))],
        out_specs=[row(d), row(d)],
        out_shape=[jax.ShapeDtypeStruct((s, d), F32), jax.ShapeDtypeStruct((s, d), BF16)],
        compiler_params=_cparams("parallel"),
        name="out_ln",
    )(ya, yb, yc, ym, wa, wb, wc, wm, x, g, b)


def _rot_half_cols(w):
    half = w.shape[-1] // 2
    return jnp.concatenate([-w[..., half:], w[..., :half]], axis=-1)


def _prep_weights(w_in, w_uq, w_ukv, w_mem_k, w_mem_v, w_out):
    depth = w_in.shape[0]
    o = [0]
    for wdt in (MLA_Q_RANK, MLA_KV_RANK, MLA_ROPE, MLA_WIDTH, SG_WIDTH, SG_WIDTH, SG_WIDTH,
                SB_WIDTH, SB_WIDTH, SB_WIDTH, SB_WIDTH, MEM_WIDTH, MEM_WIDTH):
        o.append(o[-1] + wdt)
    col = lambda i: w_in[:, :, o[i]:o[i + 1]]
    c_q, c_kv, k_pe, g_a, sg_u, sg_v, g_b, sb_q, sb_k, sb_v, g_c, m_q, g_m = [col(i) for i in range(13)]
    z64 = jnp.zeros(k_pe.shape, w_in.dtype)
    w1 = jnp.concatenate([c_q, c_kv, k_pe, z64, _rot_half_cols(k_pe), z64,
                          sg_u, sg_v, g_b, g_a, g_c, g_m], axis=-1).astype(BF16)
    w2 = jnp.concatenate([sb_k, m_q], axis=-1).astype(BF16)
    w2t = jnp.swapaxes(jnp.concatenate([sb_q, sb_v], axis=-1), 1, 2).astype(BF16)

    wq = w_uq.reshape(depth, MLA_Q_RANK, MLA_HEADS, MLA_NOPE + MLA_ROPE)
    q_nope, q_rope = wq[..., :MLA_NOPE], wq[..., MLA_NOPE:]
    zq = jnp.zeros(q_rope.shape, w_uq.dtype)
    wq_main = jnp.concatenate([q_nope, q_rope, zq], axis=-1).reshape(depth, MLA_Q_RANK, MLA_HEADS * MLA_QK_PAD)
    wq_rot = jnp.concatenate([_rot_half_cols(q_rope), zq], axis=-1).reshape(depth, MLA_Q_RANK, MLA_HEADS * LANE)
    wqt = jnp.swapaxes(jnp.concatenate([wq_main, wq_rot], axis=-1), 1, 2).astype(BF16)

    wkv = w_ukv.reshape(depth, MLA_KV_RANK, MLA_HEADS, MLA_NOPE + MLA_V)
    wk = wkv[..., :MLA_NOPE].reshape(depth, MLA_KV_RANK, MLA_WIDTH).astype(BF16)
    wvt = jnp.swapaxes(wkv[..., MLA_NOPE:].reshape(depth, MLA_KV_RANK, MLA_WIDTH), 1, 2).astype(BF16)

    wo = w_out.astype(BF16)
    e0, e1, e2 = MLA_WIDTH, MLA_WIDTH + SG_WIDTH, MLA_WIDTH + SG_WIDTH + SB_WIDTH
    return dict(w1=w1, w2=w2, w2t=w2t, wqt=wqt, wk=wk, wvt=wvt,
                wmk=w_mem_k.astype(BF16), wmv=w_mem_v.astype(BF16),
                wo_a=wo[:, :e0], wo_b=wo[:, e0:e1], wo_c=wo[:, e1:e2], wo_m=wo[:, e2:])


def kernel(x, mem, positions, w_in, q_norm_g, w_uq, kv_norm_g, w_ukv, sg_ln_g, sg_ln_b, sg_w, sg_b,
           w_mem_k, w_mem_v, w_out, ln_g, ln_b):
    b, s, d = x.shape
    depth = w_in.shape[0]
    alpha = (2.0 * depth) ** 0.25
    sb_tc = min(s, ATT_TC)

    inv_freq = ROPE_THETA ** (-jnp.arange(0, MLA_ROPE, 2, dtype=F32) / MLA_ROPE)
    invf = jnp.concatenate([inv_freq, inv_freq, jnp.zeros((LANE - MLA_ROPE,), F32)])
    tri = (lax.broadcasted_iota(jnp.int32, (sb_tc, sb_tc), 1)
           > lax.broadcasted_iota(jnp.int32, (sb_tc, sb_tc), 0)).astype(BF16)
    wts = _prep_weights(w_in, w_uq, w_ukv, w_mem_k, w_mem_v, w_out)
    sg_b_t = jnp.swapaxes(sg_b, 1, 2)

    outs = []
    for bi in range(b):
        xf = x[bi]
        xb = xf.astype(BF16)
        cos, sin, cos_t, sin_t = _rope_tables(positions[bi].reshape(s, 1), positions[bi].reshape(1, s),
                                              invf[None, :], invf[:, None])
        for l in range(depth):
            h1 = _matmul(xb, wts["w1"][l], F32, 1024, 512, "in_proj_f32")
            h2 = _matmul(xb, wts["w2"][l], BF16, 1024, H2_W, "in_proj_bf16")
            h2t = _matmul_nt(wts["w2t"][l], xb, BF16, 1024, 512, "in_proj_bf16_t")
            qt, k, vt = _mla_prep(h1, q_norm_g[l][None, :], kv_norm_g[l][None, :], wts["wqt"][l], wts["wk"][l],
                                  wts["wvt"][l], cos, sin, cos_t, sin_t)
            ya = _mla_attn(qt, k, vt, h1)
            yc = _sb_attn(h2, h2t, h1, tri)
            yb = _gmlp(h1, sg_ln_g[l][None, :], sg_ln_b[l][None, :], sg_w[l], sg_b_t[l])
            kbd, vbd = _mem_kv(mem[bi], wts["wmk"][l], wts["wmv"][l])
            ym = _mem_attn(h2, kbd, vbd, h1)
            xf, xb = _out_ln(ya, yb, yc, ym, wts["wo_a"][l], wts["wo_b"][l], wts["wo_c"][l], wts["wo_m"][l],
                             xf, ln_g[l][None, :], ln_b[l][None, :], alpha)
        outs.append(xf)
    return outs[0][None] if b == 1 else jnp.stack(outs, axis=0)
```

```python
import functools
import math

import jax
import jax.numpy as jnp
from jax import lax
from jax.experimental import pallas as pl
from jax.experimental.pallas import tpu as pltpu

F32 = jnp.float32
BF16 = jnp.bfloat16

D_MODEL = 2048
CHUNK = 64
MLA_HEADS = 6
MLA_NOPE = 128
MLA_ROPE = 64
MLA_V = 128
MLA_Q_RANK = 512
MLA_KV_RANK = 256
MLA_WIDTH = MLA_HEADS * MLA_V
ROPE_THETA = 10000.0
SG_GROUPS = 4
SG_GROUP_CH = 128
SG_WIDTH = SG_GROUPS * SG_GROUP_CH
SG_CHUNK = 128
SB_HEADS = 4
SB_HEAD_DIM = 128
SB_WIDTH = SB_HEADS * SB_HEAD_DIM
MEM_TOKENS = 256
MEM_HEADS = 4
MEM_HEAD_DIM = 64
MEM_WIDTH = MEM_HEADS * MEM_HEAD_DIM
LN_EPS = 1e-5
RMS_EPS = 1e-6

LOG2E = math.log2(math.e)
MLA_SCALE = 1.0 / math.sqrt(MLA_NOPE + MLA_ROPE)
SB_SCALE = 1.0 / math.sqrt(SB_HEAD_DIM)
MEM_SCALE = 1.0 / math.sqrt(MEM_HEAD_DIM)

LANE = 128
BF16_ROWS = 16
MXU_TILE = 256
MLA_QK_PAD = 256
MLA_V_AUG = MLA_V + BF16_ROWS
NEG_BIG = -1e30
VMEM_LIMIT = 48 * 1024 * 1024
MLA_TQ = 1024
SB_TQ = 512
ATT_TC = MXU_TILE

H1_LAT_W = 1024
H1_SGU = 1024
H1_SGV = 1536
H1_GB = 2048
H1_GA = 2560
H1_GC = 3328
H1_GM = 3840
H1_W = 4096
H2_SBK = 0
H2_MQ = 512
H2_W = 768
H2T_SBQ = 0
H2T_SBV = 512
H2T_W = 1024

_NT = (((1,), (1,)), ((), ()))


def _cparams(*sem, flags=None):
    return pltpu.CompilerParams(dimension_semantics=sem, vmem_limit_bytes=VMEM_LIMIT, flags=flags)


ATT_FLAGS = None


def _rope_table_kernel(pos_col_ref, pos_row_ref, invf_row_ref, invf_col_ref, cos_ref, sin_ref, cos_t_ref, sin_t_ref):
    ang = pos_col_ref[...].astype(F32) * invf_row_ref[...]
    cos_ref[...] = jnp.cos(ang)
    sin_ref[...] = jnp.sin(ang)
    ang_t = invf_col_ref[...] * pos_row_ref[...].astype(F32)
    cos_t_ref[...] = jnp.cos(ang_t)
    sin_t_ref[...] = jnp.sin(ang_t)


def _rope_tables(pos_col, pos_row, invf_row, invf_col):
    s = pos_col.shape[0]
    tm = min(s, 1024)
    return pl.pallas_call(
        _rope_table_kernel,
        grid=(s // tm,),
        in_specs=[pl.BlockSpec((tm, 1), lambda i: (i, 0)),
                  pl.BlockSpec((1, tm), lambda i: (0, i)),
                  pl.BlockSpec((1, LANE), lambda i: (0, 0)),
                  pl.BlockSpec((LANE, 1), lambda i: (0, 0))],
        out_specs=[pl.BlockSpec((tm, LANE), lambda i: (i, 0)),
                   pl.BlockSpec((tm, LANE), lambda i: (i, 0)),
                   pl.BlockSpec((LANE, tm), lambda i: (0, i)),
                   pl.BlockSpec((LANE, tm), lambda i: (0, i))],
        out_shape=[jax.ShapeDtypeStruct((s, LANE), F32)] * 2 + [jax.ShapeDtypeStruct((LANE, s), F32)] * 2,
        compiler_params=_cparams("parallel"),
        name="rope_tables",
    )(pos_col, pos_row, invf_row, invf_col)


_IN_SEGMENTS = ("c_q", "c_kv", "k_pe", "g_a", "sg_u", "sg_v", "g_b", "sb_q", "sb_k", "sb_v", "g_c", "m_q", "g_m")
_IN_WIDTHS = (MLA_Q_RANK, MLA_KV_RANK, MLA_ROPE, MLA_WIDTH, SG_WIDTH, SG_WIDTH, SG_WIDTH,
              SB_WIDTH, SB_WIDTH, SB_WIDTH, SB_WIDTH, MEM_WIDTH, MEM_WIDTH)
_IN_OFFSETS = tuple(sum(_IN_WIDTHS[:i]) for i in range(len(_IN_WIDTHS) + 1))
IN_COLS = {name: (_IN_OFFSETS[i], _IN_OFFSETS[i + 1]) for i, name in enumerate(_IN_SEGMENTS)}


def _columns(ref, name):
    a, b = IN_COLS[name]
    a0 = a // LANE * LANE
    b1 = -(-b // LANE) * LANE
    return ref[:, a0:b1][:, a - a0:b - a0]


def _w_in_prep_kernel(w_ref, w1_ref, w2_ref, w2q_ref):
    rows = w_ref.shape[0]
    lat = MLA_Q_RANK + MLA_KV_RANK
    w1_ref[:, 0:lat] = w_ref[:, 0:lat].astype(BF16)
    kp = w_ref[:, lat:lat + LANE]
    lane = lax.broadcasted_iota(jnp.int32, kp.shape, 1)
    w1_ref[:, lat:lat + LANE] = jnp.where(lane < MLA_ROPE, kp, 0.0).astype(BF16)
    w1_ref[:, lat + LANE:H1_LAT_W] = jnp.zeros((rows, H1_LAT_W - lat - LANE), BF16)
    for name, off in (("sg_u", H1_SGU), ("sg_v", H1_SGV), ("g_b", H1_GB), ("g_a", H1_GA), ("g_c", H1_GC),
                      ("g_m", H1_GM)):
        seg = _columns(w_ref, name)
        w1_ref[:, off:off + seg.shape[1]] = seg.astype(BF16)
    for dst, name, off in ((w2_ref, "sb_k", H2_SBK), (w2_ref, "m_q", H2_MQ),
                           (w2q_ref, "sb_q", H2T_SBQ), (w2q_ref, "sb_v", H2T_SBV)):
        seg = _columns(w_ref, name)
        dst[:, off:off + seg.shape[1]] = seg.astype(BF16)


def _w_in_prep(w_in):
    depth, k, n = w_in.shape
    tr = 256
    blk = lambda w: pl.BlockSpec((None, tr, w), lambda l, i: (l, i, 0))
    return pl.pallas_call(
        _w_in_prep_kernel,
        grid=(depth, k // tr),
        in_specs=[blk(n)],
        out_specs=[blk(H1_W), blk(H2_W), blk(H2T_W)],
        out_shape=[jax.ShapeDtypeStruct((depth, k, H1_W), BF16),
                   jax.ShapeDtypeStruct((depth, k, H2_W), BF16),
                   jax.ShapeDtypeStruct((depth, k, H2T_W), BF16)],
        compiler_params=_cparams("parallel", "parallel"),
        name="w_in_prep",
    )(w_in)


def _mm_kernel(x_ref, w_ref, o_ref):
    o_ref[...] = jnp.dot(x_ref[...], w_ref[...], preferred_element_type=F32).astype(o_ref.dtype)


def _matmul(x, w, layer, out_dtype, tm, tn, name):
    m, k = x.shape
    n = w.shape[2]
    tm = min(tm, m)
    return pl.pallas_call(
        _mm_kernel,
        grid=(m // tm, n // tn),
        in_specs=[pl.BlockSpec((tm, k), lambda i, j: (i, 0)),
                  pl.BlockSpec((None, k, tn), lambda i, j: (layer, 0, j))],
        out_specs=pl.BlockSpec((tm, tn), lambda i, j: (i, j)),
        out_shape=jax.ShapeDtypeStruct((m, n), out_dtype),
        compiler_params=_cparams("parallel", "parallel"),
        name=name,
    )(x, w)


def _mm_tn_kernel(w_ref, x_ref, scale_ref, o_ref):
    acc = lax.dot_general(w_ref[...], x_ref[...], (((0,), (1,)), ((), ())), preferred_element_type=F32)
    o_ref[...] = (acc * scale_ref[...]).astype(o_ref.dtype)


def _matmul_tn(w, layer, x, row_scale, out_dtype, tm, tn, name):
    k, n = w.shape[1:]
    m = x.shape[0]
    tm = min(tm, m)
    return pl.pallas_call(
        _mm_tn_kernel,
        grid=(m // tm, n // tn),
        in_specs=[pl.BlockSpec((None, k, tn), lambda i, j: (layer, 0, j)),
                  pl.BlockSpec((tm, k), lambda i, j: (i, 0)),
                  pl.BlockSpec((tn, 1), lambda i, j: (j, 0))],
        out_specs=pl.BlockSpec((tn, tm), lambda i, j: (j, i)),
        out_shape=jax.ShapeDtypeStruct((n, m), out_dtype),
        compiler_params=_cparams("parallel", "parallel"),
        name=name,
    )(w, x, row_scale)


def _rms(x, g):
    ms = jnp.mean(x * x, axis=-1, keepdims=True)
    return x * lax.rsqrt(ms + RMS_EPS) * g


def _mla_prep_kernel(lat_ref, qg_ref, kvg_ref, wqt_ref, wk_ref, wvt_ref, cos_ref, sin_ref, cos_t_ref, sin_t_ref,
                     qt_ref, k_ref, vt_ref):
    half = MLA_ROPE // 2
    cqn = _rms(lat_ref[:, 0:MLA_Q_RANK], qg_ref[...]).astype(BF16)
    ckvn = _rms(lat_ref[:, MLA_Q_RANK:MLA_Q_RANK + MLA_KV_RANK], kvg_ref[...]).astype(BF16)
    qa_t = lax.dot_general(wqt_ref[...], cqn, _NT, preferred_element_type=F32) * (MLA_SCALE * LOG2E)
    kn = jnp.dot(ckvn, wk_ref[...], preferred_element_type=F32)
    v_t = lax.dot_general(wvt_ref[...], ckvn, _NT, preferred_element_type=F32).astype(BF16)
    ones = jnp.ones((BF16_ROWS, v_t.shape[1]), BF16)
    cos_t = cos_t_ref[...]
    sin_t = sin_t_ref[...]
    kpe = lat_ref[:, MLA_Q_RANK + MLA_KV_RANK:MLA_Q_RANK + MLA_KV_RANK + LANE]
    lane = lax.broadcasted_iota(jnp.int32, kpe.shape, 1)
    kpe_rot = jnp.where(lane < half, -pltpu.roll(kpe, LANE - half, 1),
                        jnp.where(lane < MLA_ROPE, pltpu.roll(kpe, half, 1), 0.0))
    krot = (kpe * cos_ref[...] + kpe_rot * sin_ref[...]).astype(BF16)
    for h in range(MLA_HEADS):
        c0 = h * MLA_QK_PAD
        qt_ref[c0:c0 + LANE, :] = qa_t[c0:c0 + LANE, :].astype(BF16)
        qr = qa_t[c0 + LANE:c0 + 2 * LANE, :]
        qr_rot = jnp.concatenate([-qr[half:MLA_ROPE], qr[0:half], qr[MLA_ROPE:LANE]], axis=0)
        qt_ref[c0 + LANE:c0 + 2 * LANE, :] = (qr * cos_t + qr_rot * sin_t).astype(BF16)
        k_ref[:, c0:c0 + LANE] = kn[:, h * LANE:(h + 1) * LANE].astype(BF16)
        k_ref[:, c0 + LANE:c0 + 2 * LANE] = krot
        vt_ref[h, 0:MLA_V, :] = v_t[h * MLA_V:(h + 1) * MLA_V, :]
        vt_ref[h, MLA_V:MLA_V_AUG, :] = ones


def _mla_prep(h1, layer, qg, kvg, wqt, wk, wvt, cos, sin, cos_t, sin_t):
    s = h1.shape[0]
    tm = min(s, 512)
    nq = MLA_HEADS * MLA_QK_PAD
    full = lambda a: pl.BlockSpec((None,) + a.shape[1:], lambda i: (layer, 0, 0))
    rows = lambda w: pl.BlockSpec((tm, w), lambda i: (i, 0))
    cols = lambda w: pl.BlockSpec((w, tm), lambda i: (0, i))
    return pl.pallas_call(
        _mla_prep_kernel,
        grid=(s // tm,),
        in_specs=[rows(H1_LAT_W), full(qg), full(kvg), full(wqt), full(wk), full(wvt),
                  rows(LANE), rows(LANE), cols(LANE), cols(LANE)],
        out_specs=[cols(nq), rows(nq), pl.BlockSpec((MLA_HEADS, MLA_V_AUG, tm), lambda i: (0, 0, i))],
        out_shape=[jax.ShapeDtypeStruct((nq, s), BF16),
                   jax.ShapeDtypeStruct((s, nq), BF16),
                   jax.ShapeDtypeStruct((MLA_HEADS, MLA_V_AUG, s), BF16)],
        compiler_params=_cparams("parallel"),
        name="mla_prep",
    )(h1, qg, kvg, wqt, wk, wvt, cos, sin, cos_t, sin_t)


def _mla_attn_kernel(qt_ref, k_ref, vt_ref, g_ref, o_ref, m_sc, acc_sc, s_buf, p_buf, al_buf, *, tq, tc):
    n = pl.program_id(1) + 1
    m_sc[...] = jnp.full(m_sc.shape, NEG_BIG, F32)
    acc_sc[...] = jnp.zeros(acc_sc.shape, F32)
    col_tiles = [slice(c * tc, (c + 1) * tc) for c in range(tq // tc)]

    def stage_scores(j, slot):
        k = k_ref[pl.ds(pl.multiple_of(j * tq, tq), tq), :]
        for cols in col_tiles:
            s_buf[slot, :, cols] = jnp.dot(k, qt_ref[:, cols], preferred_element_type=F32)

    def stage_softmax(slot, masked):
        for c, cols in enumerate(col_tiles):
            n_keys = (c + 1) * tc if masked else tq
            s = s_buf[slot, 0:n_keys, cols]
            if masked:
                kc = lax.broadcasted_iota(jnp.int32, s.shape, 0) // CHUNK
                qc = (lax.broadcasted_iota(jnp.int32, s.shape, 1) + c * tc) // CHUNK
                s = jnp.where(kc <= qc, s, NEG_BIG)
            m_prev = m_sc[:, cols]
            m_new = jnp.maximum(m_prev, jnp.max(s, axis=0, keepdims=True))
            p_buf[slot, 0:n_keys, cols] = jnp.exp2(s - m_new).astype(BF16)
            if n_keys < tq:
                p_buf[slot, n_keys:tq, cols] = jnp.zeros((tq - n_keys, tc), BF16)
            al_buf[slot, :, cols] = jnp.exp2(m_prev - m_new)
            m_sc[:, cols] = m_new

    def stage_values(j, slot):
        vt = vt_ref[:, pl.ds(pl.multiple_of(j * tq, tq), tq)]
        for cols in col_tiles:
            pv = jnp.dot(vt, p_buf[slot, :, cols], preferred_element_type=F32)
            acc_sc[:, cols] = al_buf[slot, :, cols] * acc_sc[:, cols] + pv

    def trip(i, slot):
        stage_scores(i, slot)
        stage_values(i - 2, slot)
        stage_softmax(1 - slot, False)

    def drain(last_slot):
        stage_values(n - 2, 1 - last_slot)
        stage_softmax(last_slot, True)
        stage_values(n - 1, last_slot)

    stage_scores(0, 0)

    @pl.when(n >= 2)
    def _():
        stage_softmax(0, False)
        stage_scores(1, 1)

    def body(ii, carry):
        i = 2 + 2 * ii
        trip(i, 0)
        trip(i + 1, 1)
        return carry

    lax.fori_loop(0, (n - 2) // 2, body, 0)
    n_odd = n % 2 == 1

    @pl.when(jnp.logical_and(n_odd, n >= 3))
    def _():
        trip(n - 1, 0)
        drain(0)

    @pl.when(jnp.logical_not(n_odd))
    def _():
        drain(1)

    @pl.when(n == 1)
    def _():
        stage_softmax(0, True)
        stage_values(0, 0)

    o = (acc_sc[0:MLA_V, :] / acc_sc[MLA_V:MLA_V + 1, :]).T
    o_ref[...] = (o * jax.nn.silu(g_ref[...])).astype(o_ref.dtype)


def _mla_attn(qt, k, vt, h1):
    s = k.shape[0]
    tq = min(MLA_TQ, s)
    tc = min(ATT_TC, tq)
    ga_blk = H1_GA // LANE
    return pl.pallas_call(
        functools.partial(_mla_attn_kernel, tq=tq, tc=tc),
        grid=(MLA_HEADS, s // tq),
        in_specs=[pl.BlockSpec((MLA_QK_PAD, tq), lambda h, i: (h, i)),
                  pl.BlockSpec((s, MLA_QK_PAD), lambda h, i: (0, h)),
                  pl.BlockSpec((None, MLA_V_AUG, s), lambda h, i: (h, 0, 0)),
                  pl.BlockSpec((tq, LANE), lambda h, i: (i, ga_blk + h))],
        out_specs=pl.BlockSpec((tq, MLA_V), lambda h, i: (i, h)),
        out_shape=jax.ShapeDtypeStruct((s, MLA_WIDTH), BF16),
        scratch_shapes=[pltpu.VMEM((1, tq), F32), pltpu.VMEM((MLA_V_AUG, tq), F32),
                        pltpu.VMEM((2, tq, tq), F32), pltpu.VMEM((2, tq, tq), BF16), pltpu.VMEM((2, 1, tq), F32)],
        compiler_params=_cparams("parallel", "parallel", flags=ATT_FLAGS),
        name="mla_attn",
    )(qt, k, vt, h1)


def _sb_attn_kernel(qt_ref, k_ref, vt_ref, tri_ref, g_ref, o_ref, r_sc, acc_sc,
                    lb_buf, hi_buf, x_buf, cs_buf, a_buf, *, tq, tc):
    n = pl.program_id(1) + 1
    n_sub = tq // tc
    r_sc[...] = jnp.zeros(r_sc.shape, F32)
    acc_sc[...] = jnp.zeros(acc_sc.shape, F32)
    sub = [slice(t * tc, (t + 1) * tc) for t in range(n_sub)]
    strict = (lax.broadcasted_iota(jnp.int32, (tc, tc), 0)
              < lax.broadcasted_iota(jnp.int32, (tc, tc), 1))

    def stage_a(u, slot, diag):
        start = (n - 1 - u) * tq
        for kk in range(n_sub):
            k = k_ref[pl.ds(pl.multiple_of(start + kk * tc, tc), tc), :]
            for c in range(n_sub):
                if diag and kk > c:
                    hi_buf[slot, sub[kk], sub[c]] = jnp.zeros((tc, tc), BF16)
                    continue
                y = jnp.dot(k, qt_ref[:, sub[c]], preferred_element_type=F32)
                t = jnp.log(1.0 + jnp.exp2(-jnp.abs(y))) * LOG2E
                neg_part = jnp.minimum(y, 0.0)
                log_1mb = (neg_part - y) - t
                if diag and kk == c:
                    log_1mb = jnp.where(strict, log_1mb, 0.0)
                lb_buf[slot, sub[kk], sub[c]] = neg_part - t
                hi_buf[slot, sub[kk], sub[c]] = log_1mb.astype(BF16)

    def stage_b(slot):
        tri = tri_ref[...]
        for kk in range(n_sub):
            for c in range(n_sub):
                r = jnp.dot(tri, hi_buf[slot, sub[kk], sub[c]], preferred_element_type=F32)
                x_buf[slot, sub[kk], sub[c]] = r[:tc] + lb_buf[slot, sub[kk], sub[c]]
                cs_buf[slot, kk, :, sub[c]] = r[tc:tc + 1]

    def stage_c_weights(slot, diag):
        for c in range(n_sub):
            run = r_sc[:, sub[c]]
            for kk in range(n_sub - 1, -1, -1):
                if diag and kk > c:
                    a_buf[sub[kk], sub[c]] = jnp.zeros((tc, tc), BF16)
                    continue
                a = jnp.exp2(x_buf[slot, sub[kk], sub[c]] + run)
                if diag and kk == c:
                    a = jnp.where(strict, a, 0.0)
                a_buf[sub[kk], sub[c]] = a.astype(BF16)
                run = run + cs_buf[slot, kk, :, sub[c]]
            r_sc[:, sub[c]] = run

    def stage_c_values(u):
        start = pl.multiple_of((n - 1 - u) * tq, tq)
        vt = vt_ref[:, pl.ds(start, tq)]
        for c in range(n_sub):
            acc_sc[:, sub[c]] += jnp.dot(vt, a_buf[:, sub[c]], preferred_element_type=F32)

    def stage_c(u, slot, diag):
        stage_c_weights(slot, diag)
        stage_c_values(u)

    def trip(i, slot):
        stage_c_weights(slot, False)
        stage_a(i, slot, False)
        stage_b(1 - slot)
        stage_c_values(i - 2)

    stage_a(0, 0, True)

    @pl.when(n >= 2)
    def _():
        stage_a(1, 1, False)

    stage_b(0)

    @pl.when(n >= 3)
    def _():
        stage_a(2, 0, False)

    @pl.when(n >= 2)
    def _():
        stage_b(1)

    stage_c(0, 0, True)

    def body(ii, carry):
        i = 3 + 2 * ii
        trip(i, 1)
        trip(i + 1, 0)
        return carry

    lax.fori_loop(0, jnp.maximum(n - 3, 0) // 2, body, 0)

    @pl.when(jnp.logical_and(n % 2 == 0, n >= 4))
    def _():
        trip(n - 1, 1)

    last = (n - 1) % 2

    @pl.when(n >= 3)
    def _():
        stage_b(last)
        stage_c(n - 2, 1 - last, False)

    @pl.when(n >= 2)
    def _():
        stage_c(n - 1, last, False)

    o_ref[...] = (acc_sc[...].T * jax.nn.silu(g_ref[...])).astype(o_ref.dtype)


def _sb_attn(h2, h2t, h1, tri):
    s = h2.shape[0]
    tq = min(SB_TQ, s)
    tc = tri.shape[1]
    qb, kb, vb, gb = H2T_SBQ // LANE, H2_SBK // LANE, H2T_SBV // LANE, H1_GC // LANE
    return pl.pallas_call(
        functools.partial(_sb_attn_kernel, tq=tq, tc=tc),
        grid=(SB_HEADS, s // tq),
        in_specs=[pl.BlockSpec((LANE, tq), lambda h, i: (qb + h, i)),
                  pl.BlockSpec((s, LANE), lambda h, i: (0, kb + h)),
                  pl.BlockSpec((LANE, s), lambda h, i: (vb + h, 0)),
                  pl.BlockSpec(tri.shape, lambda h, i: (0, 0)),
                  pl.BlockSpec((tq, LANE), lambda h, i: (i, gb + h))],
        out_specs=pl.BlockSpec((tq, LANE), lambda h, i: (i, h)),
        out_shape=jax.ShapeDtypeStruct((s, SB_WIDTH), BF16),
        scratch_shapes=[pltpu.VMEM((1, tq), F32), pltpu.VMEM((SB_HEAD_DIM, tq), F32),
                        pltpu.VMEM((2, tq, tq), F32), pltpu.VMEM((2, tq, tq), BF16),
                        pltpu.VMEM((2, tq, tq), F32), pltpu.VMEM((2, tq // tc, 1, tq), F32),
                        pltpu.VMEM((tq, tq), BF16)],
        compiler_params=_cparams("parallel", "parallel", flags=ATT_FLAGS),
        name="sb_attn",
    )(h2t, h2, h2t, tri, h1)


def _gmlp_kernel(u_ref, v_ref, g_ref, lng_ref, lnb_ref, w_ref, b_ref, o_ref, *, n_chunks):
    gv = jax.nn.gelu(v_ref[...])
    mu = jnp.mean(gv, axis=-1, keepdims=True)
    vc = gv - mu
    var = jnp.mean(vc * vc, axis=-1, keepdims=True)
    vn = (vc * lax.rsqrt(var + LN_EPS) * lng_ref[...] + lnb_ref[...]).astype(BF16)
    t_chunk = lax.broadcasted_iota(jnp.int32, (SG_CHUNK, SG_CHUNK), 0) // CHUNK
    s_chunk = lax.broadcasted_iota(jnp.int32, (SG_CHUNK, SG_CHUNK), 1) // CHUNK
    mask = (s_chunk <= t_chunk).astype(F32)
    for g in range(SG_GROUPS):
        w_sp = (w_ref[g] * mask).astype(BF16)
        bias = b_ref[:, g:g + 1]
        cols = slice(g * SG_GROUP_CH, (g + 1) * SG_GROUP_CH)
        for n in range(n_chunks):
            rows = slice(n * SG_CHUNK, (n + 1) * SG_CHUNK)
            mixed = jnp.dot(w_sp, vn[rows, cols], preferred_element_type=F32) + bias
            o_b = jax.nn.gelu(u_ref[rows, cols]) * mixed
            o_ref[rows, cols] = (o_b * jax.nn.silu(g_ref[rows, cols])).astype(o_ref.dtype)


def _gmlp(h1, layer, lng, lnb, w, b_t):
    s = h1.shape[0]
    tm = min(s, 512)
    full = lambda a: pl.BlockSpec((None,) + a.shape[1:], lambda i: (layer,) + (0,) * (a.ndim - 1))
    return pl.pallas_call(
        functools.partial(_gmlp_kernel, n_chunks=tm // SG_CHUNK),
        grid=(s // tm,),
        in_specs=[pl.BlockSpec((tm, SG_WIDTH), lambda i: (i, H1_SGU // SG_WIDTH)),
                  pl.BlockSpec((tm, SG_WIDTH), lambda i: (i, H1_SGV // SG_WIDTH)),
                  pl.BlockSpec((tm, SG_WIDTH), lambda i: (i, H1_GB // SG_WIDTH)),
                  full(lng), full(lnb), full(w), full(b_t)],
        out_specs=pl.BlockSpec((tm, SG_WIDTH), lambda i: (i, 0)),
        out_shape=jax.ShapeDtypeStruct((s, SG_WIDTH), BF16),
        compiler_params=_cparams("parallel"),
        name="gmlp",
    )(h1, h1, h1, lng, lnb, w, b_t)


def _mem_kv_kernel(mem_ref, wk_ref, wv_ref, kbd_ref, vbd_ref):
    mem = mem_ref[...].astype(BF16)
    mk_t = jnp.dot(mem, wk_ref[...], preferred_element_type=F32).T
    mv = jnp.dot(mem, wv_ref[...], preferred_element_type=F32)
    feat_row = lax.broadcasted_iota(jnp.int32, mk_t.shape, 0) // MEM_HEAD_DIM
    feat_col = lax.broadcasted_iota(jnp.int32, mv.shape, 1) // MEM_HEAD_DIM
    for h in range(MEM_HEADS):
        seg = slice(h * MEM_TOKENS, (h + 1) * MEM_TOKENS)
        kbd_ref[:, seg] = jnp.where(feat_row == h, mk_t, 0.0).astype(BF16)
        vbd_ref[seg, :] = jnp.where(feat_col == h, mv, 0.0).astype(BF16)


def _mem_kv(mem, layer, wk, wv):
    nt = MEM_HEADS * MEM_TOKENS
    full2 = lambda shape: pl.BlockSpec(shape, lambda i: (0, 0))
    stack = lambda a: pl.BlockSpec((None,) + a.shape[1:], lambda i: (layer, 0, 0))
    return pl.pallas_call(
        _mem_kv_kernel,
        grid=(1,),
        in_specs=[full2(mem.shape), stack(wk), stack(wv)],
        out_specs=[full2((MEM_WIDTH, nt)), full2((nt, MEM_WIDTH))],
        out_shape=[jax.ShapeDtypeStruct((MEM_WIDTH, nt), BF16),
                   jax.ShapeDtypeStruct((nt, MEM_WIDTH), BF16)],
        compiler_params=_cparams("arbitrary"),
        name="mem_kv",
    )(mem, wk, wv)


def _mem_attn_kernel(q_ref, kbd_ref, vbd_ref, g_ref, o_ref):
    logits = jnp.dot(q_ref[...], kbd_ref[...], preferred_element_type=F32) * MEM_SCALE
    probs = []
    for h in range(MEM_HEADS):
        seg = logits[:, h * MEM_TOKENS:(h + 1) * MEM_TOKENS]
        e = jnp.exp(seg - jnp.max(seg, axis=-1, keepdims=True))
        probs.append((e / jnp.sum(e, axis=-1, keepdims=True)).astype(BF16))
    p = jnp.concatenate(probs, axis=-1)
    o = jnp.dot(p, vbd_ref[...], preferred_element_type=F32)
    o_ref[...] = (o * jax.nn.silu(g_ref[...])).astype(o_ref.dtype)


def _mem_attn(h2, kbd, vbd, h1):
    s = h2.shape[0]
    tm = min(s, 512)
    full2 = lambda shape: pl.BlockSpec(shape, lambda i: (0, 0))
    return pl.pallas_call(
        _mem_attn_kernel,
        grid=(s // tm,),
        in_specs=[pl.BlockSpec((tm, MEM_WIDTH), lambda i: (i, H2_MQ // MEM_WIDTH)),
                  full2(kbd.shape), full2(vbd.shape),
                  pl.BlockSpec((tm, MEM_WIDTH), lambda i: (i, H1_GM // MEM_WIDTH))],
        out_specs=pl.BlockSpec((tm, MEM_WIDTH), lambda i: (i, 0)),
        out_shape=jax.ShapeDtypeStruct((s, MEM_WIDTH), BF16),
        compiler_params=_cparams("parallel"),
        name="mem_attn",
    )(h2, kbd, vbd, h1)


def _out_ln_kernel(ya_ref, yb_ref, yc_ref, ym_ref, w_ref, x_ref, g_ref, b_ref, o_ref, obf_ref, *, alpha):
    y = None
    row = 0
    for y_ref in (ya_ref, yb_ref, yc_ref, ym_ref):
        width = y_ref.shape[1]
        part = jnp.dot(y_ref[...], w_ref[row:row + width, :], preferred_element_type=F32)
        y = part if y is None else y + part
        row += width
    r = alpha * x_ref[...] + y
    mu = jnp.mean(r, axis=-1, keepdims=True)
    rc = r - mu
    var = jnp.mean(rc * rc, axis=-1, keepdims=True)
    out = rc * lax.rsqrt(var + LN_EPS) * g_ref[...] + b_ref[...]
    o_ref[...] = out
    obf_ref[...] = out.astype(BF16)


def _out_ln(ya, yb, yc, ym, w, layer, x, g, b, alpha):
    s, d = x.shape
    tm = min(s, 256)
    row = lambda w: pl.BlockSpec((tm, w), lambda i: (i, 0))
    full = lambda a: pl.BlockSpec((None,) + a.shape[1:], lambda i: (layer, 0, 0))
    return pl.pallas_call(
        functools.partial(_out_ln_kernel, alpha=alpha),
        grid=(s // tm,),
        in_specs=[row(MLA_WIDTH), row(SG_WIDTH), row(SB_WIDTH), row(MEM_WIDTH),
                  full(w), row(d), full(g), full(b)],
        out_specs=[row(d), row(d)],
        out_shape=[jax.ShapeDtypeStruct((s, d), F32), jax.ShapeDtypeStruct((s, d), BF16)],
        compiler_params=_cparams("parallel"),
        name="out_ln",
    )(ya, yb, yc, ym, w, x, g, b)


def _prep_mla_weights(w_uq, w_ukv):
    depth = w_uq.shape[0]
    wq = w_uq.reshape(depth, MLA_Q_RANK, MLA_HEADS, MLA_NOPE + MLA_ROPE)
    wq = jnp.pad(wq, ((0, 0), (0, 0), (0, 0), (0, MLA_QK_PAD - MLA_NOPE - MLA_ROPE)))
    wqt = jnp.swapaxes(wq.reshape(depth, MLA_Q_RANK, MLA_HEADS * MLA_QK_PAD), 1, 2).astype(BF16)
    wkv = w_ukv.reshape(depth, MLA_KV_RANK, MLA_HEADS, MLA_NOPE + MLA_V)
    wk = wkv[..., :MLA_NOPE].reshape(depth, MLA_KV_RANK, MLA_WIDTH).astype(BF16)
    wvt = jnp.swapaxes(wkv[..., MLA_NOPE:].reshape(depth, MLA_KV_RANK, MLA_WIDTH), 1, 2).astype(BF16)
    return wqt, wk, wvt


def kernel(x, mem, positions, w_in, q_norm_g, w_uq, kv_norm_g, w_ukv, sg_ln_g, sg_ln_b, sg_w, sg_b,
           w_mem_k, w_mem_v, w_out, ln_g, ln_b):
    b, s, d = x.shape
    depth = w_in.shape[0]
    alpha = (2.0 * depth) ** 0.25
    sb_tc = min(s, ATT_TC)

    inv_freq = ROPE_THETA ** (-jnp.arange(0, MLA_ROPE, 2, dtype=F32) / MLA_ROPE)
    invf = jnp.concatenate([inv_freq, inv_freq, jnp.zeros((LANE - MLA_ROPE,), F32)])
    tri = jnp.concatenate(
        [(lax.broadcasted_iota(jnp.int32, (sb_tc, sb_tc), 1)
          > lax.broadcasted_iota(jnp.int32, (sb_tc, sb_tc), 0)).astype(BF16),
         jnp.ones((BF16_ROWS, sb_tc), BF16)], axis=0)
    w1, w2, w2q = _w_in_prep(w_in)
    wqt, wk, wvt = _prep_mla_weights(w_uq, w_ukv)
    wmk, wmv, wo = w_mem_k.astype(BF16), w_mem_v.astype(BF16), w_out.astype(BF16)
    row3 = lambda p: p[:, None, :]
    sg_b_t = jnp.swapaxes(sg_b, 1, 2)
    h2t_scale = jnp.concatenate([jnp.full((SB_WIDTH, 1), SB_SCALE * LOG2E, F32),
                                 jnp.ones((SB_WIDTH, 1), F32)], axis=0)

    outs = []
    for bi in range(b):
        xf = x[bi]
        xb = xf.astype(BF16)
        cos, sin, cos_t, sin_t = _rope_tables(positions[bi].reshape(s, 1), positions[bi].reshape(1, s),
                                              invf[None, :], invf[:, None])
        for l in range(depth):
            h1 = _matmul(xb, w1, l, F32, 1024, 512, "in_proj_f32")
            h2 = _matmul(xb, w2, l, BF16, 1024, H2_W, "in_proj_bf16")
            h2t = _matmul_tn(w2q, l, xb, h2t_scale, BF16, 1024, 512, "in_proj_bf16_t")
            qt, k, vt = _mla_prep(h1, l, row3(q_norm_g), row3(kv_norm_g), wqt, wk, wvt, cos, sin, cos_t, sin_t)
            ya = _mla_attn(qt, k, vt, h1)
            yc = _sb_attn(h2, h2t, h1, tri)
            yb = _gmlp(h1, l, row3(sg_ln_g), row3(sg_ln_b), sg_w, sg_b_t)
            kbd, vbd = _mem_kv(mem[bi], l, wmk, wmv)
            ym = _mem_attn(h2, kbd, vbd, h1)
            xf, xb = _out_ln(ya, yb, yc, ym, wo, l, xf, row3(ln_g), row3(ln_b), alpha)
        outs.append(xf)
    return outs[0][None] if b == 1 else jnp.stack(outs, axis=0)
```

```python
import functools
import math

import jax
import jax.numpy as jnp
from jax import lax
from jax.experimental import pallas as pl
from jax.experimental.pallas import tpu as pltpu

F32 = jnp.float32
BF16 = jnp.bfloat16

D_MODEL = 2048
CHUNK = 64
MLA_HEADS = 6
MLA_NOPE = 128
MLA_ROPE = 64
MLA_V = 128
MLA_Q_RANK = 512
MLA_KV_RANK = 256
MLA_WIDTH = MLA_HEADS * MLA_V
ROPE_THETA = 10000.0
SG_GROUPS = 4
SG_GROUP_CH = 128
SG_WIDTH = SG_GROUPS * SG_GROUP_CH
SG_CHUNK = 128
SB_HEADS = 4
SB_HEAD_DIM = 128
SB_WIDTH = SB_HEADS * SB_HEAD_DIM
MEM_TOKENS = 256
MEM_HEADS = 4
MEM_HEAD_DIM = 64
MEM_WIDTH = MEM_HEADS * MEM_HEAD_DIM
LN_EPS = 1e-5
RMS_EPS = 1e-6

LOG2E = math.log2(math.e)
MLA_SCALE = 1.0 / math.sqrt(MLA_NOPE + MLA_ROPE)
SB_SCALE = 1.0 / math.sqrt(SB_HEAD_DIM)
MEM_SCALE = 1.0 / math.sqrt(MEM_HEAD_DIM)

LANE = 128
BF16_ROWS = 16
MXU_TILE = 256
MLA_QK_PAD = 256
MLA_V_AUG = MLA_V + BF16_ROWS
NEG_BIG = -1e30
VMEM_LIMIT = 48 * 1024 * 1024
MLA_TQ = 1024
SB_TQ = 512
ATT_TC = MXU_TILE

H1_LAT_W = 1024
H1_SGU = 1024
H1_SGV = 1536
H1_GB = 2048
H1_GA = 2560
H1_GC = 3328
H1_GM = 3840
H1_W = 4096
H2_SBK = 0
H2_MQ = 512
H2_W = 768
H2T_SBQ = 0
H2T_SBV = 512
H2T_W = 1024

_NT = (((1,), (1,)), ((), ()))


def _cparams(*sem, flags=None):
    return pltpu.CompilerParams(dimension_semantics=sem, vmem_limit_bytes=VMEM_LIMIT, flags=flags)


ATT_FLAGS = None


def _rope_table_kernel(pos_col_ref, pos_row_ref, invf_row_ref, invf_col_ref, cos_ref, sin_ref, cos_t_ref, sin_t_ref):
    ang = pos_col_ref[...].astype(F32) * invf_row_ref[...]
    cos_ref[...] = jnp.cos(ang)
    sin_ref[...] = jnp.sin(ang)
    ang_t = invf_col_ref[...] * pos_row_ref[...].astype(F32)
    cos_t_ref[...] = jnp.cos(ang_t)
    sin_t_ref[...] = jnp.sin(ang_t)


def _rope_tables(pos_col, pos_row, invf_row, invf_col):
    s = pos_col.shape[0]
    tm = min(s, 1024)
    return pl.pallas_call(
        _rope_table_kernel,
        grid=(s // tm,),
        in_specs=[pl.BlockSpec((tm, 1), lambda i: (i, 0)),
                  pl.BlockSpec((1, tm), lambda i: (0, i)),
                  pl.BlockSpec((1, LANE), lambda i: (0, 0)),
                  pl.BlockSpec((LANE, 1), lambda i: (0, 0))],
        out_specs=[pl.BlockSpec((tm, LANE), lambda i: (i, 0)),
                   pl.BlockSpec((tm, LANE), lambda i: (i, 0)),
                   pl.BlockSpec((LANE, tm), lambda i: (0, i)),
                   pl.BlockSpec((LANE, tm), lambda i: (0, i))],
        out_shape=[jax.ShapeDtypeStruct((s, LANE), F32)] * 2 + [jax.ShapeDtypeStruct((LANE, s), F32)] * 2,
        compiler_params=_cparams("parallel"),
        name="rope_tables",
    )(pos_col, pos_row, invf_row, invf_col)


_IN_SEGMENTS = ("c_q", "c_kv", "k_pe", "g_a", "sg_u", "sg_v", "g_b", "sb_q", "sb_k", "sb_v", "g_c", "m_q", "g_m")
_IN_WIDTHS = (MLA_Q_RANK, MLA_KV_RANK, MLA_ROPE, MLA_WIDTH, SG_WIDTH, SG_WIDTH, SG_WIDTH,
              SB_WIDTH, SB_WIDTH, SB_WIDTH, SB_WIDTH, MEM_WIDTH, MEM_WIDTH)
_IN_OFFSETS = tuple(sum(_IN_WIDTHS[:i]) for i in range(len(_IN_WIDTHS) + 1))
IN_ROWS = {name: (_IN_OFFSETS[i], _IN_OFFSETS[i + 1]) for i, name in enumerate(_IN_SEGMENTS)}


def _w_in_prep_kernel(wt_ref, w1_ref, w2_ref, w2q_ref):
    def put(dst, off, name):
        a, b = IN_ROWS[name]
        dst[off:off + b - a, :] = wt_ref[a:b, :].astype(BF16)

    lat_end = MLA_Q_RANK + MLA_KV_RANK + MLA_ROPE
    for name, off in (("c_q", 0), ("c_kv", MLA_Q_RANK), ("k_pe", MLA_Q_RANK + MLA_KV_RANK),
                      ("sg_u", H1_SGU), ("sg_v", H1_SGV), ("g_b", H1_GB), ("g_a", H1_GA), ("g_c", H1_GC),
                      ("g_m", H1_GM)):
        put(w1_ref, off, name)
    w1_ref[lat_end:H1_LAT_W, :] = jnp.zeros((H1_LAT_W - lat_end, w1_ref.shape[1]), BF16)
    put(w2_ref, H2_SBK, "sb_k")
    put(w2_ref, H2_MQ, "m_q")
    put(w2q_ref, H2T_SBQ, "sb_q")
    put(w2q_ref, H2T_SBV, "sb_v")


def _w_in_prep(w_in_t):
    depth, n, k = w_in_t.shape
    tk = 256
    blk = lambda rows: pl.BlockSpec((None, rows, tk), lambda l, i: (l, 0, i))
    return pl.pallas_call(
        _w_in_prep_kernel,
        grid=(depth, k // tk),
        in_specs=[blk(n)],
        out_specs=[blk(H1_W), blk(H2_W), blk(H2T_W)],
        out_shape=[jax.ShapeDtypeStruct((depth, H1_W, k), BF16),
                   jax.ShapeDtypeStruct((depth, H2_W, k), BF16),
                   jax.ShapeDtypeStruct((depth, H2T_W, k), BF16)],
        compiler_params=_cparams("parallel", "parallel"),
        name="w_in_prep",
    )(w_in_t)


def _mm_kernel(x_ref, wt_ref, o_ref):
    o_ref[...] = lax.dot_general(x_ref[...], wt_ref[...], _NT, preferred_element_type=F32).astype(o_ref.dtype)


def _matmul(x, wt, layer, out_dtype, tm, tn, name):
    m, k = x.shape
    n = wt.shape[1]
    tm = min(tm, m)
    return pl.pallas_call(
        _mm_kernel,
        grid=(m // tm, n // tn),
        in_specs=[pl.BlockSpec((tm, k), lambda i, j: (i, 0)),
                  pl.BlockSpec((None, tn, k), lambda i, j: (layer, j, 0))],
        out_specs=pl.BlockSpec((tm, tn), lambda i, j: (i, j)),
        out_shape=jax.ShapeDtypeStruct((m, n), out_dtype),
        compiler_params=_cparams("parallel", "parallel"),
        name=name,
    )(x, wt)


def _mm_nt_kernel(wt_ref, x_ref, scale_ref, o_ref):
    acc = lax.dot_general(wt_ref[...], x_ref[...], _NT, preferred_element_type=F32)
    o_ref[...] = (acc * scale_ref[...]).astype(o_ref.dtype)


def _matmul_nt(wt, layer, x, row_scale, out_dtype, tm, tn, name):
    n, k = wt.shape[1:]
    m = x.shape[0]
    tm = min(tm, m)
    return pl.pallas_call(
        _mm_nt_kernel,
        grid=(m // tm, n // tn),
        in_specs=[pl.BlockSpec((None, tn, k), lambda i, j: (layer, j, 0)),
                  pl.BlockSpec((tm, k), lambda i, j: (i, 0)),
                  pl.BlockSpec((tn, 1), lambda i, j: (j, 0))],
        out_specs=pl.BlockSpec((tn, tm), lambda i, j: (j, i)),
        out_shape=jax.ShapeDtypeStruct((n, m), out_dtype),
        compiler_params=_cparams("parallel", "parallel"),
        name=name,
    )(wt, x, row_scale)


def _rms(x, g):
    ms = jnp.mean(x * x, axis=-1, keepdims=True)
    return x * lax.rsqrt(ms + RMS_EPS) * g


def _mla_prep_kernel(lat_ref, qg_ref, kvg_ref, wqt_ref, wk_ref, wvt_ref, cos_ref, sin_ref, cos_t_ref, sin_t_ref,
                     qt_ref, k_ref, vt_ref):
    half = MLA_ROPE // 2
    cqn = _rms(lat_ref[:, 0:MLA_Q_RANK], qg_ref[...]).astype(BF16)
    ckvn = _rms(lat_ref[:, MLA_Q_RANK:MLA_Q_RANK + MLA_KV_RANK], kvg_ref[...]).astype(BF16)
    qa_t = lax.dot_general(wqt_ref[...], cqn, _NT, preferred_element_type=F32) * (MLA_SCALE * LOG2E)
    kn = jnp.dot(ckvn, wk_ref[...], preferred_element_type=F32)
    v_t = lax.dot_general(wvt_ref[...], ckvn, _NT, preferred_element_type=F32).astype(BF16)
    ones = jnp.ones((BF16_ROWS, v_t.shape[1]), BF16)
    cos_t = cos_t_ref[...]
    sin_t = sin_t_ref[...]
    kpe = lat_ref[:, MLA_Q_RANK + MLA_KV_RANK:MLA_Q_RANK + MLA_KV_RANK + LANE]
    lane = lax.broadcasted_iota(jnp.int32, kpe.shape, 1)
    kpe_rot = jnp.where(lane < half, -pltpu.roll(kpe, LANE - half, 1),
                        jnp.where(lane < MLA_ROPE, pltpu.roll(kpe, half, 1), 0.0))
    krot = (kpe * cos_ref[...] + kpe_rot * sin_ref[...]).astype(BF16)
    for h in range(MLA_HEADS):
        c0 = h * MLA_QK_PAD
        qt_ref[c0:c0 + LANE, :] = qa_t[c0:c0 + LANE, :].astype(BF16)
        qr = qa_t[c0 + LANE:c0 + 2 * LANE, :]
        qr_rot = jnp.concatenate([-qr[half:MLA_ROPE], qr[0:half], qr[MLA_ROPE:LANE]], axis=0)
        qt_ref[c0 + LANE:c0 + 2 * LANE, :] = (qr * cos_t + qr_rot * sin_t).astype(BF16)
        k_ref[:, c0:c0 + LANE] = kn[:, h * LANE:(h + 1) * LANE].astype(BF16)
        k_ref[:, c0 + LANE:c0 + 2 * LANE] = krot
        vt_ref[h, 0:MLA_V, :] = v_t[h * MLA_V:(h + 1) * MLA_V, :]
        vt_ref[h, MLA_V:MLA_V_AUG, :] = ones


def _mla_prep(h1, layer, qg, kvg, wqt, wk, wvt, cos, sin, cos_t, sin_t):
    s = h1.shape[0]
    tm = min(s, 512)
    nq = MLA_HEADS * MLA_QK_PAD
    full = lambda a: pl.BlockSpec((None,) + a.shape[1:], lambda i: (layer, 0, 0))
    rows = lambda w: pl.BlockSpec((tm, w), lambda i: (i, 0))
    cols = lambda w: pl.BlockSpec((w, tm), lambda i: (0, i))
    return pl.pallas_call(
        _mla_prep_kernel,
        grid=(s // tm,),
        in_specs=[rows(H1_LAT_W), full(qg), full(kvg), full(wqt), full(wk), full(wvt),
                  rows(LANE), rows(LANE), cols(LANE), cols(LANE)],
        out_specs=[cols(nq), rows(nq), pl.BlockSpec((MLA_HEADS, MLA_V_AUG, tm), lambda i: (0, 0, i))],
        out_shape=[jax.ShapeDtypeStruct((nq, s), BF16),
                   jax.ShapeDtypeStruct((s, nq), BF16),
                   jax.ShapeDtypeStruct((MLA_HEADS, MLA_V_AUG, s), BF16)],
        compiler_params=_cparams("parallel"),
        name="mla_prep",
    )(h1, qg, kvg, wqt, wk, wvt, cos, sin, cos_t, sin_t)


def _mla_attn_kernel(qt_ref, k_ref, vt_ref, g_ref, o_ref, m_sc, acc_sc, s_buf, p_buf, al_buf, *, tq, tc):
    n = pl.program_id(1) + 1
    m_sc[...] = jnp.full(m_sc.shape, NEG_BIG, F32)
    acc_sc[...] = jnp.zeros(acc_sc.shape, F32)
    col_tiles = [slice(c * tc, (c + 1) * tc) for c in range(tq // tc)]

    def stage_scores(j, slot):
        k = k_ref[pl.ds(pl.multiple_of(j * tq, tq), tq), :]
        for cols in col_tiles:
            s_buf[slot, :, cols] = jnp.dot(k, qt_ref[:, cols], preferred_element_type=F32)

    def stage_softmax(slot, masked):
        for c, cols in enumerate(col_tiles):
            n_keys = (c + 1) * tc if masked else tq
            s = s_buf[slot, 0:n_keys, cols]
            if masked:
                kc = lax.broadcasted_iota(jnp.int32, s.shape, 0) // CHUNK
                qc = (lax.broadcasted_iota(jnp.int32, s.shape, 1) + c * tc) // CHUNK
                s = jnp.where(kc <= qc, s, NEG_BIG)
            m_prev = m_sc[:, cols]
            m_new = jnp.maximum(m_prev, jnp.max(s, axis=0, keepdims=True))
            p_buf[slot, 0:n_keys, cols] = jnp.exp2(s - m_new).astype(BF16)
            if n_keys < tq:
                p_buf[slot, n_keys:tq, cols] = jnp.zeros((tq - n_keys, tc), BF16)
            al_buf[slot, :, cols] = jnp.exp2(m_prev - m_new)
            m_sc[:, cols] = m_new

    def stage_values(j, slot):
        vt = vt_ref[:, pl.ds(pl.multiple_of(j * tq, tq), tq)]
        for cols in col_tiles:
            pv = jnp.dot(vt, p_buf[slot, :, cols], preferred_element_type=F32)
            acc_sc[:, cols] = al_buf[slot, :, cols] * acc_sc[:, cols] + pv

    def trip(i, slot):
        stage_scores(i, slot)
        stage_values(i - 2, slot)
        stage_softmax(1 - slot, False)

    def drain(last_slot):
        stage_values(n - 2, 1 - last_slot)
        stage_softmax(last_slot, True)
        stage_values(n - 1, last_slot)

    stage_scores(0, 0)

    @pl.when(n >= 2)
    def _():
        stage_softmax(0, False)
        stage_scores(1, 1)

    def body(ii, carry):
        i = 2 + 2 * ii
        trip(i, 0)
        trip(i + 1, 1)
        return carry

    lax.fori_loop(0, (n - 2) // 2, body, 0)
    n_odd = n % 2 == 1

    @pl.when(jnp.logical_and(n_odd, n >= 3))
    def _():
        trip(n - 1, 0)
        drain(0)

    @pl.when(jnp.logical_not(n_odd))
    def _():
        drain(1)

    @pl.when(n == 1)
    def _():
        stage_softmax(0, True)
        stage_values(0, 0)

    o = (acc_sc[0:MLA_V, :] / acc_sc[MLA_V:MLA_V + 1, :]).T
    o_ref[...] = (o * jax.nn.silu(g_ref[...])).astype(o_ref.dtype)


def _mla_attn(qt, k, vt, h1):
    s = k.shape[0]
    tq = min(MLA_TQ, s)
    tc = min(ATT_TC, tq)
    ga_blk = H1_GA // LANE
    return pl.pallas_call(
        functools.partial(_mla_attn_kernel, tq=tq, tc=tc),
        grid=(MLA_HEADS, s // tq),
        in_specs=[pl.BlockSpec((MLA_QK_PAD, tq), lambda h, i: (h, i)),
                  pl.BlockSpec((s, MLA_QK_PAD), lambda h, i: (0, h)),
                  pl.BlockSpec((None, MLA_V_AUG, s), lambda h, i: (h, 0, 0)),
                  pl.BlockSpec((tq, LANE), lambda h, i: (i, ga_blk + h))],
        out_specs=pl.BlockSpec((tq, MLA_V), lambda h, i: (i, h)),
        out_shape=jax.ShapeDtypeStruct((s, MLA_WIDTH), BF16),
        scratch_shapes=[pltpu.VMEM((1, tq), F32), pltpu.VMEM((MLA_V_AUG, tq), F32),
                        pltpu.VMEM((2, tq, tq), F32), pltpu.VMEM((2, tq, tq), BF16), pltpu.VMEM((2, 1, tq), F32)],
        compiler_params=_cparams("parallel", "parallel", flags=ATT_FLAGS),
        name="mla_attn",
    )(qt, k, vt, h1)


def _sb_attn_kernel(qt_ref, k_ref, vt_ref, tri_ref, g_ref, o_ref, r_sc, acc_sc,
                    lb_buf, hi_buf, x_buf, cs_buf, a_buf, *, tq, tc):
    n = pl.program_id(1) + 1
    n_sub = tq // tc
    r_sc[...] = jnp.zeros(r_sc.shape, F32)
    acc_sc[...] = jnp.zeros(acc_sc.shape, F32)
    sub = [slice(t * tc, (t + 1) * tc) for t in range(n_sub)]
    strict = (lax.broadcasted_iota(jnp.int32, (tc, tc), 0)
              < lax.broadcasted_iota(jnp.int32, (tc, tc), 1))

    def stage_a(u, slot, diag):
        start = (n - 1 - u) * tq
        for kk in range(n_sub):
            k = k_ref[pl.ds(pl.multiple_of(start + kk * tc, tc), tc), :]
            for c in range(n_sub):
                if diag and kk > c:
                    hi_buf[slot, sub[kk], sub[c]] = jnp.zeros((tc, tc), BF16)
                    continue
                y = jnp.dot(k, qt_ref[:, sub[c]], preferred_element_type=F32)
                t = jnp.log(1.0 + jnp.exp2(-jnp.abs(y))) * LOG2E
                neg_part = jnp.minimum(y, 0.0)
                log_1mb = (neg_part - y) - t
                if diag and kk == c:
                    log_1mb = jnp.where(strict, log_1mb, 0.0)
                lb_buf[slot, sub[kk], sub[c]] = neg_part - t
                hi_buf[slot, sub[kk], sub[c]] = log_1mb.astype(BF16)

    def stage_b(slot):
        tri = tri_ref[...]
        for kk in range(n_sub):
            for c in range(n_sub):
                r = jnp.dot(tri, hi_buf[slot, sub[kk], sub[c]], preferred_element_type=F32)
                x_buf[slot, sub[kk], sub[c]] = r[:tc] + lb_buf[slot, sub[kk], sub[c]]
                cs_buf[slot, kk, :, sub[c]] = r[tc:tc + 1]

    def stage_c_weights(slot, diag):
        for c in range(n_sub):
            run = r_sc[:, sub[c]]
            for kk in range(n_sub - 1, -1, -1):
                if diag and kk > c:
                    a_buf[sub[kk], sub[c]] = jnp.zeros((tc, tc), BF16)
                    continue
                a = jnp.exp2(x_buf[slot, sub[kk], sub[c]] + run)
                if diag and kk == c:
                    a = jnp.where(strict, a, 0.0)
                a_buf[sub[kk], sub[c]] = a.astype(BF16)
                run = run + cs_buf[slot, kk, :, sub[c]]
            r_sc[:, sub[c]] = run

    def stage_c_values(u):
        start = pl.multiple_of((n - 1 - u) * tq, tq)
        vt = vt_ref[:, pl.ds(start, tq)]
        for c in range(n_sub):
            acc_sc[:, sub[c]] += jnp.dot(vt, a_buf[:, sub[c]], preferred_element_type=F32)

    def stage_c(u, slot, diag):
        stage_c_weights(slot, diag)
        stage_c_values(u)

    def trip(i, slot):
        stage_c_weights(slot, False)
        stage_a(i, slot, False)
        stage_c_values(i - 2)
        stage_b(1 - slot)

    stage_a(0, 0, True)

    @pl.when(n >= 2)
    def _():
        stage_a(1, 1, False)

    stage_b(0)

    @pl.when(n >= 3)
    def _():
        stage_a(2, 0, False)

    @pl.when(n >= 2)
    def _():
        stage_b(1)

    stage_c(0, 0, True)

    def body(ii, carry):
        i = 3 + 2 * ii
        trip(i, 1)
        trip(i + 1, 0)
        return carry

    lax.fori_loop(0, jnp.maximum(n - 3, 0) // 2, body, 0)

    @pl.when(jnp.logical_and(n % 2 == 0, n >= 4))
    def _():
        trip(n - 1, 1)

    last = (n - 1) % 2

    @pl.when(n >= 3)
    def _():
        stage_b(last)
        stage_c(n - 2, 1 - last, False)

    @pl.when(n >= 2)
    def _():
        stage_c(n - 1, last, False)

    o_ref[...] = (acc_sc[...].T * jax.nn.silu(g_ref[...])).astype(o_ref.dtype)


def _sb_attn(h2, h2t, h1, tri):
    s = h2.shape[0]
    tq = min(SB_TQ, s)
    tc = tri.shape[1]
    qb, kb, vb, gb = H2T_SBQ // LANE, H2_SBK // LANE, H2T_SBV // LANE, H1_GC // LANE
    return pl.pallas_call(
        functools.partial(_sb_attn_kernel, tq=tq, tc=tc),
        grid=(SB_HEADS, s // tq),
        in_specs=[pl.BlockSpec((LANE, tq), lambda h, i: (qb + h, i)),
                  pl.BlockSpec((s, LANE), lambda h, i: (0, kb + h)),
                  pl.BlockSpec((LANE, s), lambda h, i: (vb + h, 0)),
                  pl.BlockSpec(tri.shape, lambda h, i: (0, 0)),
                  pl.BlockSpec((tq, LANE), lambda h, i: (i, gb + h))],
        out_specs=pl.BlockSpec((tq, LANE), lambda h, i: (i, h)),
        out_shape=jax.ShapeDtypeStruct((s, SB_WIDTH), BF16),
        scratch_shapes=[pltpu.VMEM((1, tq), F32), pltpu.VMEM((SB_HEAD_DIM, tq), F32),
                        pltpu.VMEM((2, tq, tq), F32), pltpu.VMEM((2, tq, tq), BF16),
                        pltpu.VMEM((2, tq, tq), F32), pltpu.VMEM((2, tq // tc, 1, tq), F32),
                        pltpu.VMEM((tq, tq), BF16)],
        compiler_params=_cparams("parallel", "parallel", flags=ATT_FLAGS),
        name="sb_attn",
    )(h2t, h2, h2t, tri, h1)


def _gmlp_kernel(u_ref, v_ref, g_ref, lng_ref, lnb_ref, w_ref, b_ref, o_ref, *, n_chunks):
    gv = jax.nn.gelu(v_ref[...])
    mu = jnp.mean(gv, axis=-1, keepdims=True)
    vc = gv - mu
    var = jnp.mean(vc * vc, axis=-1, keepdims=True)
    vn = (vc * lax.rsqrt(var + LN_EPS) * lng_ref[...] + lnb_ref[...]).astype(BF16)
    t_chunk = lax.broadcasted_iota(jnp.int32, (SG_CHUNK, SG_CHUNK), 0) // CHUNK
    s_chunk = lax.broadcasted_iota(jnp.int32, (SG_CHUNK, SG_CHUNK), 1) // CHUNK
    mask = (s_chunk <= t_chunk).astype(F32)
    for g in range(SG_GROUPS):
        w_sp = (w_ref[g] * mask).astype(BF16)
        bias = b_ref[:, g:g + 1]
        cols = slice(g * SG_GROUP_CH, (g + 1) * SG_GROUP_CH)
        for n in range(n_chunks):
            rows = slice(n * SG_CHUNK, (n + 1) * SG_CHUNK)
            mixed = jnp.dot(w_sp, vn[rows, cols], preferred_element_type=F32) + bias
            o_b = jax.nn.gelu(u_ref[rows, cols]) * mixed
            o_ref[rows, cols] = (o_b * jax.nn.silu(g_ref[rows, cols])).astype(o_ref.dtype)


def _gmlp(h1, layer, lng, lnb, w, b_t):
    s = h1.shape[0]
    tm = min(s, 512)
    full = lambda a: pl.BlockSpec((None,) + a.shape[1:], lambda i: (layer,) + (0,) * (a.ndim - 1))
    return pl.pallas_call(
        functools.partial(_gmlp_kernel, n_chunks=tm // SG_CHUNK),
        grid=(s // tm,),
        in_specs=[pl.BlockSpec((tm, SG_WIDTH), lambda i: (i, H1_SGU // SG_WIDTH)),
                  pl.BlockSpec((tm, SG_WIDTH), lambda i: (i, H1_SGV // SG_WIDTH)),
                  pl.BlockSpec((tm, SG_WIDTH), lambda i: (i, H1_GB // SG_WIDTH)),
                  full(lng), full(lnb), full(w), full(b_t)],
        out_specs=pl.BlockSpec((tm, SG_WIDTH), lambda i: (i, 0)),
        out_shape=jax.ShapeDtypeStruct((s, SG_WIDTH), BF16),
        compiler_params=_cparams("parallel"),
        name="gmlp",
    )(h1, h1, h1, lng, lnb, w, b_t)


def _mem_kv_kernel(mem_ref, wk_ref, wv_ref, kbd_ref, vbd_ref):
    mem = mem_ref[...].astype(BF16)
    mk_t = jnp.dot(mem, wk_ref[...], preferred_element_type=F32).T
    mv = jnp.dot(mem, wv_ref[...], preferred_element_type=F32)
    feat_row = lax.broadcasted_iota(jnp.int32, mk_t.shape, 0) // MEM_HEAD_DIM
    feat_col = lax.broadcasted_iota(jnp.int32, mv.shape, 1) // MEM_HEAD_DIM
    for h in range(MEM_HEADS):
        seg = slice(h * MEM_TOKENS, (h + 1) * MEM_TOKENS)
        kbd_ref[:, seg] = jnp.where(feat_row == h, mk_t, 0.0).astype(BF16)
        vbd_ref[seg, :] = jnp.where(feat_col == h, mv, 0.0).astype(BF16)


def _mem_kv(mem, layer, wk, wv):
    nt = MEM_HEADS * MEM_TOKENS
    full2 = lambda shape: pl.BlockSpec(shape, lambda i: (0, 0))
    stack = lambda a: pl.BlockSpec((None,) + a.shape[1:], lambda i: (layer, 0, 0))
    return pl.pallas_call(
        _mem_kv_kernel,
        grid=(1,),
        in_specs=[full2(mem.shape), stack(wk), stack(wv)],
        out_specs=[full2((MEM_WIDTH, nt)), full2((nt, MEM_WIDTH))],
        out_shape=[jax.ShapeDtypeStruct((MEM_WIDTH, nt), BF16),
                   jax.ShapeDtypeStruct((nt, MEM_WIDTH), BF16)],
        compiler_params=_cparams("arbitrary"),
        name="mem_kv",
    )(mem, wk, wv)


def _mem_attn_kernel(q_ref, kbd_ref, vbd_ref, g_ref, o_ref):
    logits = jnp.dot(q_ref[...], kbd_ref[...], preferred_element_type=F32) * MEM_SCALE
    probs = []
    for h in range(MEM_HEADS):
        seg = logits[:, h * MEM_TOKENS:(h + 1) * MEM_TOKENS]
        e = jnp.exp(seg - jnp.max(seg, axis=-1, keepdims=True))
        probs.append((e / jnp.sum(e, axis=-1, keepdims=True)).astype(BF16))
    p = jnp.concatenate(probs, axis=-1)
    o = jnp.dot(p, vbd_ref[...], preferred_element_type=F32)
    o_ref[...] = (o * jax.nn.silu(g_ref[...])).astype(o_ref.dtype)


def _mem_attn(h2, kbd, vbd, h1):
    s = h2.shape[0]
    tm = min(s, 512)
    full2 = lambda shape: pl.BlockSpec(shape, lambda i: (0, 0))
    return pl.pallas_call(
        _mem_attn_kernel,
        grid=(s // tm,),
        in_specs=[pl.BlockSpec((tm, MEM_WIDTH), lambda i: (i, H2_MQ // MEM_WIDTH)),
                  full2(kbd.shape), full2(vbd.shape),
                  pl.BlockSpec((tm, MEM_WIDTH), lambda i: (i, H1_GM // MEM_WIDTH))],
        out_specs=pl.BlockSpec((tm, MEM_WIDTH), lambda i: (i, 0)),
        out_shape=jax.ShapeDtypeStruct((s, MEM_WIDTH), BF16),
        compiler_params=_cparams("parallel"),
        name="mem_attn",
    )(h2, kbd, vbd, h1)


def _out_ln_kernel(ya_ref, yb_ref, yc_ref, ym_ref, w_ref, x_ref, g_ref, b_ref, o_ref, obf_ref, *, alpha):
    y = None
    row = 0
    for y_ref in (ya_ref, yb_ref, yc_ref, ym_ref):
        width = y_ref.shape[1]
        part = jnp.dot(y_ref[...], w_ref[row:row + width, :], preferred_element_type=F32)
        y = part if y is None else y + part
        row += width
    r = alpha * x_ref[...] + y
    mu = jnp.mean(r, axis=-1, keepdims=True)
    rc = r - mu
    var = jnp.mean(rc * rc, axis=-1, keepdims=True)
    out = rc * lax.rsqrt(var + LN_EPS) * g_ref[...] + b_ref[...]
    o_ref[...] = out
    obf_ref[...] = out.astype(BF16)


def _out_ln(ya, yb, yc, ym, w, layer, x, g, b, alpha):
    s, d = x.shape
    tm = min(s, 512)
    row = lambda w: pl.BlockSpec((tm, w), lambda i: (i, 0))
    full = lambda a: pl.BlockSpec((None,) + a.shape[1:], lambda i: (layer, 0, 0))
    return pl.pallas_call(
        functools.partial(_out_ln_kernel, alpha=alpha),
        grid=(s // tm,),
        in_specs=[row(MLA_WIDTH), row(SG_WIDTH), row(SB_WIDTH), row(MEM_WIDTH),
                  full(w), row(d), full(g), full(b)],
        out_specs=[row(d), row(d)],
        out_shape=[jax.ShapeDtypeStruct((s, d), F32), jax.ShapeDtypeStruct((s, d), BF16)],
        compiler_params=_cparams("parallel"),
        name="out_ln",
    )(ya, yb, yc, ym, w, x, g, b)


def _prep_mla_weights(w_uq, w_ukv):
    depth = w_uq.shape[0]
    wq = w_uq.reshape(depth, MLA_Q_RANK, MLA_HEADS, MLA_NOPE + MLA_ROPE)
    wq = jnp.pad(wq, ((0, 0), (0, 0), (0, 0), (0, MLA_QK_PAD - MLA_NOPE - MLA_ROPE)))
    wqt = jnp.swapaxes(wq.reshape(depth, MLA_Q_RANK, MLA_HEADS * MLA_QK_PAD), 1, 2).astype(BF16)
    wkv = w_ukv.reshape(depth, MLA_KV_RANK, MLA_HEADS, MLA_NOPE + MLA_V)
    wk = wkv[..., :MLA_NOPE].reshape(depth, MLA_KV_RANK, MLA_WIDTH).astype(BF16)
    wvt = jnp.swapaxes(wkv[..., MLA_NOPE:].reshape(depth, MLA_KV_RANK, MLA_WIDTH), 1, 2).astype(BF16)
    return wqt, wk, wvt


def kernel(x, mem, positions, w_in, q_norm_g, w_uq, kv_norm_g, w_ukv, sg_ln_g, sg_ln_b, sg_w, sg_b,
           w_mem_k, w_mem_v, w_out, ln_g, ln_b):
    b, s, d = x.shape
    depth = w_in.shape[0]
    alpha = (2.0 * depth) ** 0.25
    sb_tc = min(s, ATT_TC)

    inv_freq = ROPE_THETA ** (-jnp.arange(0, MLA_ROPE, 2, dtype=F32) / MLA_ROPE)
    invf = jnp.concatenate([inv_freq, inv_freq, jnp.zeros((LANE - MLA_ROPE,), F32)])
    tri = jnp.concatenate(
        [(lax.broadcasted_iota(jnp.int32, (sb_tc, sb_tc), 1)
          > lax.broadcasted_iota(jnp.int32, (sb_tc, sb_tc), 0)).astype(BF16),
         jnp.ones((BF16_ROWS, sb_tc), BF16)], axis=0)
    w1, w2, w2q = _w_in_prep(jnp.swapaxes(w_in, 1, 2))
    wqt, wk, wvt = _prep_mla_weights(w_uq, w_ukv)
    wmk, wmv, wo = w_mem_k.astype(BF16), w_mem_v.astype(BF16), w_out.astype(BF16)
    row3 = lambda p: p[:, None, :]
    sg_b_t = jnp.swapaxes(sg_b, 1, 2)
    h2t_scale = jnp.concatenate([jnp.full((SB_WIDTH, 1), SB_SCALE * LOG2E, F32),
                                 jnp.ones((SB_WIDTH, 1), F32)], axis=0)

    outs = []
    for bi in range(b):
        xf = x[bi]
        xb = xf.astype(BF16)
        cos, sin, cos_t, sin_t = _rope_tables(positions[bi].reshape(s, 1), positions[bi].reshape(1, s),
                                              invf[None, :], invf[:, None])
        for l in range(depth):
            h1 = _matmul(xb, w1, l, F32, 1024, 1024, "in_proj_f32")
            h2 = _matmul(xb, w2, l, BF16, 1024, H2_W, "in_proj_bf16")
            h2t = _matmul_nt(w2q, l, xb, h2t_scale, BF16, 1024, H2T_W, "in_proj_bf16_t")
            qt, k, vt = _mla_prep(h1, l, row3(q_norm_g), row3(kv_norm_g), wqt, wk, wvt, cos, sin, cos_t, sin_t)
            ya = _mla_attn(qt, k, vt, h1)
            yc = _sb_attn(h2, h2t, h1, tri)
            yb = _gmlp(h1, l, row3(sg_ln_g), row3(sg_ln_b), sg_w, sg_b_t)
            kbd, vbd = _mem_kv(mem[bi], l, wmk, wmv)
            ym = _mem_attn(h2, kbd, vbd, h1)
            xf, xb = _out_ln(ya, yb, yc, ym, wo, l, xf, row3(ln_g), row3(ln_b), alpha)
        outs.append(xf)
    return outs[0][None] if b == 1 else jnp.stack(outs, axis=0)
```

```python
import functools
import math

import jax
import jax.numpy as jnp
from jax import lax
from jax.experimental import pallas as pl
from jax.experimental.pallas import tpu as pltpu

F32 = jnp.float32
BF16 = jnp.bfloat16

D_MODEL = 2048
CHUNK = 64
MLA_HEADS = 6
MLA_NOPE = 128
MLA_ROPE = 64
MLA_V = 128
MLA_Q_RANK = 512
MLA_KV_RANK = 256
MLA_WIDTH = MLA_HEADS * MLA_V
ROPE_THETA = 10000.0
SG_GROUPS = 4
SG_GROUP_CH = 128
SG_WIDTH = SG_GROUPS * SG_GROUP_CH
SG_CHUNK = 128
SB_HEADS = 4
SB_HEAD_DIM = 128
SB_WIDTH = SB_HEADS * SB_HEAD_DIM
MEM_TOKENS = 256
MEM_HEADS = 4
MEM_HEAD_DIM = 64
MEM_WIDTH = MEM_HEADS * MEM_HEAD_DIM
LN_EPS = 1e-5
RMS_EPS = 1e-6

LOG2E = math.log2(math.e)
MLA_SCALE = 1.0 / math.sqrt(MLA_NOPE + MLA_ROPE)
SB_SCALE = 1.0 / math.sqrt(SB_HEAD_DIM)
MEM_SCALE = 1.0 / math.sqrt(MEM_HEAD_DIM)

LANE = 128
BF16_ROWS = 16
MXU_TILE = 256
MLA_QK_PAD = 256
MLA_V_AUG = MLA_V + BF16_ROWS
NEG_BIG = -1e30
VMEM_LIMIT = 48 * 1024 * 1024
MLA_TQ = 1024
SB_TQ = 512
ATT_TC = MXU_TILE

H1_LAT_W = 1024
H1_SGU = 1024
H1_SGV = 1536
H1_GB = 2048
H1_GA = 2560
H1_GC = 3328
H1_GM = 3840
H1_W = 4096
H2_SBK = 0
H2_MQ = 512
H2_W = 768
H2T_SBQ = 0
H2T_SBV = 512
H2T_W = 1024

_NT = (((1,), (1,)), ((), ()))


def _cparams(*sem, flags=None):
    return pltpu.CompilerParams(dimension_semantics=sem, vmem_limit_bytes=VMEM_LIMIT, flags=flags)


ATT_FLAGS = None


def _rope_table_kernel(pos_col_ref, pos_row_ref, invf_row_ref, invf_col_ref, cos_ref, sin_ref, cos_t_ref, sin_t_ref):
    ang = pos_col_ref[...].astype(F32) * invf_row_ref[...]
    cos_ref[...] = jnp.cos(ang)
    sin_ref[...] = jnp.sin(ang)
    ang_t = invf_col_ref[...] * pos_row_ref[...].astype(F32)
    cos_t_ref[...] = jnp.cos(ang_t)
    sin_t_ref[...] = jnp.sin(ang_t)


def _rope_tables(pos_col, pos_row, invf_row, invf_col):
    s = pos_col.shape[0]
    tm = min(s, 1024)
    return pl.pallas_call(
        _rope_table_kernel,
        grid=(s // tm,),
        in_specs=[pl.BlockSpec((tm, 1), lambda i: (i, 0)),
                  pl.BlockSpec((1, tm), lambda i: (0, i)),
                  pl.BlockSpec((1, LANE), lambda i: (0, 0)),
                  pl.BlockSpec((LANE, 1), lambda i: (0, 0))],
        out_specs=[pl.BlockSpec((tm, LANE), lambda i: (i, 0)),
                   pl.BlockSpec((tm, LANE), lambda i: (i, 0)),
                   pl.BlockSpec((LANE, tm), lambda i: (0, i)),
                   pl.BlockSpec((LANE, tm), lambda i: (0, i))],
        out_shape=[jax.ShapeDtypeStruct((s, LANE), F32)] * 2 + [jax.ShapeDtypeStruct((LANE, s), F32)] * 2,
        compiler_params=_cparams("parallel"),
        name="rope_tables",
    )(pos_col, pos_row, invf_row, invf_col)


_IN_SEGMENTS = ("c_q", "c_kv", "k_pe", "g_a", "sg_u", "sg_v", "g_b", "sb_q", "sb_k", "sb_v", "g_c", "m_q", "g_m")
_IN_WIDTHS = (MLA_Q_RANK, MLA_KV_RANK, MLA_ROPE, MLA_WIDTH, SG_WIDTH, SG_WIDTH, SG_WIDTH,
              SB_WIDTH, SB_WIDTH, SB_WIDTH, SB_WIDTH, MEM_WIDTH, MEM_WIDTH)
_IN_OFFSETS = tuple(sum(_IN_WIDTHS[:i]) for i in range(len(_IN_WIDTHS) + 1))
IN_ROWS = {name: (_IN_OFFSETS[i], _IN_OFFSETS[i + 1]) for i, name in enumerate(_IN_SEGMENTS)}


def _w_in_prep_kernel(wt_ref, w1_ref, w2_ref, w2q_ref):
    def put(dst, off, name):
        a, b = IN_ROWS[name]
        dst[off:off + b - a, :] = wt_ref[a:b, :].astype(BF16)

    lat_end = MLA_Q_RANK + MLA_KV_RANK + MLA_ROPE
    for name, off in (("c_q", 0), ("c_kv", MLA_Q_RANK), ("k_pe", MLA_Q_RANK + MLA_KV_RANK),
                      ("sg_u", H1_SGU), ("sg_v", H1_SGV), ("g_b", H1_GB), ("g_a", H1_GA), ("g_c", H1_GC),
                      ("g_m", H1_GM)):
        put(w1_ref, off, name)
    w1_ref[lat_end:H1_LAT_W, :] = jnp.zeros((H1_LAT_W - lat_end, w1_ref.shape[1]), BF16)
    put(w2_ref, H2_SBK, "sb_k")
    put(w2_ref, H2_MQ, "m_q")
    put(w2q_ref, H2T_SBQ, "sb_q")
    put(w2q_ref, H2T_SBV, "sb_v")


def _w_in_prep(w_in_t):
    depth, n, k = w_in_t.shape
    tk = 256
    blk = lambda rows: pl.BlockSpec((None, rows, tk), lambda l, i: (l, 0, i))
    return pl.pallas_call(
        _w_in_prep_kernel,
        grid=(depth, k // tk),
        in_specs=[blk(n)],
        out_specs=[blk(H1_W), blk(H2_W), blk(H2T_W)],
        out_shape=[jax.ShapeDtypeStruct((depth, H1_W, k), BF16),
                   jax.ShapeDtypeStruct((depth, H2_W, k), BF16),
                   jax.ShapeDtypeStruct((depth, H2T_W, k), BF16)],
        compiler_params=_cparams("parallel", "parallel"),
        name="w_in_prep",
    )(w_in_t)


def _mm_kernel(x_ref, wt_ref, o_ref):
    o_ref[...] = lax.dot_general(x_ref[...], wt_ref[...], _NT, preferred_element_type=F32).astype(o_ref.dtype)


def _matmul(x, wt, layer, out_dtype, tm, tn, name):
    m, k = x.shape
    n = wt.shape[1]
    tm = min(tm, m)
    return pl.pallas_call(
        _mm_kernel,
        grid=(m // tm, n // tn),
        in_specs=[pl.BlockSpec((tm, k), lambda i, j: (i, 0)),
                  pl.BlockSpec((None, tn, k), lambda i, j: (layer, j, 0))],
        out_specs=pl.BlockSpec((tm, tn), lambda i, j: (i, j)),
        out_shape=jax.ShapeDtypeStruct((m, n), out_dtype),
        compiler_params=_cparams("parallel", "parallel"),
        name=name,
    )(x, wt)


def _mm_nt_kernel(wt_ref, x_ref, scale_ref, o_ref):
    acc = lax.dot_general(wt_ref[...], x_ref[...], _NT, preferred_element_type=F32)
    o_ref[...] = (acc * scale_ref[...]).astype(o_ref.dtype)


def _matmul_nt(wt, layer, x, row_scale, out_dtype, tm, tn, name):
    n, k = wt.shape[1:]
    m = x.shape[0]
    tm = min(tm, m)
    return pl.pallas_call(
        _mm_nt_kernel,
        grid=(m // tm, n // tn),
        in_specs=[pl.BlockSpec((None, tn, k), lambda i, j: (layer, j, 0)),
                  pl.BlockSpec((tm, k), lambda i, j: (i, 0)),
                  pl.BlockSpec((tn, 1), lambda i, j: (j, 0))],
        out_specs=pl.BlockSpec((tn, tm), lambda i, j: (j, i)),
        out_shape=jax.ShapeDtypeStruct((n, m), out_dtype),
        compiler_params=_cparams("parallel", "parallel"),
        name=name,
    )(wt, x, row_scale)


def _rms(x, g):
    ms = jnp.mean(x * x, axis=-1, keepdims=True)
    return x * lax.rsqrt(ms + RMS_EPS) * g


def _mla_prep_kernel(lat_ref, qg_ref, kvg_ref, wqt_ref, wk_ref, wvt_ref, cos_ref, sin_ref, cos_t_ref, sin_t_ref,
                     qt_ref, k_ref, vt_ref):
    half = MLA_ROPE // 2
    cqn = _rms(lat_ref[:, 0:MLA_Q_RANK], qg_ref[...]).astype(BF16)
    ckvn = _rms(lat_ref[:, MLA_Q_RANK:MLA_Q_RANK + MLA_KV_RANK], kvg_ref[...]).astype(BF16)
    qa_t = lax.dot_general(wqt_ref[...], cqn, _NT, preferred_element_type=F32) * (MLA_SCALE * LOG2E)
    kn = jnp.dot(ckvn, wk_ref[...], preferred_element_type=F32)
    v_t = lax.dot_general(wvt_ref[...], ckvn, _NT, preferred_element_type=F32).astype(BF16)
    ones = jnp.ones((BF16_ROWS, v_t.shape[1]), BF16)
    cos_t = cos_t_ref[...]
    sin_t = sin_t_ref[...]
    kpe = lat_ref[:, MLA_Q_RANK + MLA_KV_RANK:MLA_Q_RANK + MLA_KV_RANK + LANE]
    lane = lax.broadcasted_iota(jnp.int32, kpe.shape, 1)
    kpe_rot = jnp.where(lane < half, -pltpu.roll(kpe, LANE - half, 1),
                        jnp.where(lane < MLA_ROPE, pltpu.roll(kpe, half, 1), 0.0))
    krot = (kpe * cos_ref[...] + kpe_rot * sin_ref[...]).astype(BF16)
    for h in range(MLA_HEADS):
        c0 = h * MLA_QK_PAD
        qt_ref[c0:c0 + LANE, :] = qa_t[c0:c0 + LANE, :].astype(BF16)
        qr = qa_t[c0 + LANE:c0 + 2 * LANE, :]
        qr_rot = jnp.concatenate([-qr[half:MLA_ROPE], qr[0:half], qr[MLA_ROPE:LANE]], axis=0)
        qt_ref[c0 + LANE:c0 + 2 * LANE, :] = (qr * cos_t + qr_rot * sin_t).astype(BF16)
        k_ref[:, c0:c0 + LANE] = kn[:, h * LANE:(h + 1) * LANE].astype(BF16)
        k_ref[:, c0 + LANE:c0 + 2 * LANE] = krot
        vt_ref[h, 0:MLA_V, :] = v_t[h * MLA_V:(h + 1) * MLA_V, :]
        vt_ref[h, MLA_V:MLA_V_AUG, :] = ones


def _mla_prep(h1, layer, qg, kvg, wqt, wk, wvt, cos, sin, cos_t, sin_t):
    s = h1.shape[0]
    tm = min(s, 512)
    nq = MLA_HEADS * MLA_QK_PAD
    full = lambda a: pl.BlockSpec((None,) + a.shape[1:], lambda i: (layer, 0, 0))
    rows = lambda w: pl.BlockSpec((tm, w), lambda i: (i, 0))
    cols = lambda w: pl.BlockSpec((w, tm), lambda i: (0, i))
    return pl.pallas_call(
        _mla_prep_kernel,
        grid=(s // tm,),
        in_specs=[rows(H1_LAT_W), full(qg), full(kvg), full(wqt), full(wk), full(wvt),
                  rows(LANE), rows(LANE), cols(LANE), cols(LANE)],
        out_specs=[cols(nq), rows(nq), pl.BlockSpec((MLA_HEADS, MLA_V_AUG, tm), lambda i: (0, 0, i))],
        out_shape=[jax.ShapeDtypeStruct((nq, s), BF16),
                   jax.ShapeDtypeStruct((s, nq), BF16),
                   jax.ShapeDtypeStruct((MLA_HEADS, MLA_V_AUG, s), BF16)],
        compiler_params=_cparams("parallel"),
        name="mla_prep",
    )(h1, qg, kvg, wqt, wk, wvt, cos, sin, cos_t, sin_t)


def _run_pipeline(n_items, stage_a, stage_b, stage_c_weights, stage_c_values):
    if n_items == 0:
        return

    def stage_c(t, slot):
        stage_c_weights(t, slot)
        stage_c_values(t, slot)

    def trip(i, slot):
        stage_c_weights(i - 2, slot)
        stage_a(i, slot)
        stage_c_values(i - 2, slot)
        stage_b(i - 1, 1 - slot)

    stage_a(0, 0)
    if n_items >= 2:
        stage_a(1, 1)
    stage_b(0, 0)
    steady = max(n_items - 2, 0)

    def body(ii, carry):
        i = 2 + 2 * ii
        trip(i, 0)
        trip(i + 1, 1)
        return carry

    lax.fori_loop(0, steady // 2, body, 0)
    if steady % 2:
        trip(n_items - 1, (n_items - 1) % 2)
    if n_items >= 2:
        stage_c_weights(n_items - 2, n_items % 2)
        stage_b(n_items - 1, (n_items - 1) % 2)
        stage_c_values(n_items - 2, n_items % 2)
    stage_c(n_items - 1, (n_items - 1) % 2)


def _mla_attn_kernel(tab_ref, qt_ref, k_ref, vt_ref, g_ref, o_ref, m_st, acc_st, s_buf, p_buf, al_buf,
                     *, tq, tc, nq):
    n_off = nq * (nq - 1) // 2
    m_st[...] = jnp.full(m_st.shape, NEG_BIG, F32)
    acc_st[...] = jnp.zeros(acc_st.shape, F32)
    n_col = tq // tc
    col_tiles = [slice(c * tc, (c + 1) * tc) for c in range(n_col)]

    def q_cols(qi, c):
        return pl.ds(pl.multiple_of(qi * tq + c * tc, tc), tc)

    def stage_scores(qi, j, slot):
        k = k_ref[pl.ds(pl.multiple_of(j * tq, tq), tq), :]
        for c in range(n_col):
            s_buf[slot, :, col_tiles[c]] = jnp.dot(k, qt_ref[:, q_cols(qi, c)], preferred_element_type=F32)

    def stage_softmax(qi, slot, masked):
        for c, cols in enumerate(col_tiles):
            n_keys = (c + 1) * tc if masked else tq
            s = s_buf[slot, 0:n_keys, cols]
            if masked:
                kc = lax.broadcasted_iota(jnp.int32, s.shape, 0) // CHUNK
                qc = (lax.broadcasted_iota(jnp.int32, s.shape, 1) + c * tc) // CHUNK
                s = jnp.where(kc <= qc, s, NEG_BIG)
            m_prev = m_st[:, q_cols(qi, c)]
            m_new = jnp.maximum(m_prev, jnp.max(s, axis=0, keepdims=True))
            p_buf[slot, 0:n_keys, cols] = jnp.exp2(s - m_new).astype(BF16)
            if n_keys < tq:
                p_buf[slot, n_keys:tq, cols] = jnp.zeros((tq - n_keys, tc), BF16)
            al_buf[slot, :, cols] = jnp.exp2(m_prev - m_new)
            m_st[:, q_cols(qi, c)] = m_new

    def stage_values(qi, j, slot):
        vt = vt_ref[:, pl.ds(pl.multiple_of(j * tq, tq), tq)]
        for c, cols in enumerate(col_tiles):
            pv = jnp.dot(vt, p_buf[slot, :, cols], preferred_element_type=F32)
            acc_st[:, q_cols(qi, c)] = al_buf[slot, :, cols] * acc_st[:, q_cols(qi, c)] + pv

    nothing = lambda t, slot: None
    _run_pipeline(nq,
                  lambda t, slot: stage_scores(t, t, slot),
                  lambda t, slot: stage_softmax(t, slot, True),
                  nothing,
                  lambda t, slot: stage_values(t, t, slot))
    _run_pipeline(n_off,
                  lambda t, slot: stage_scores(tab_ref[0, t], tab_ref[1, t], slot),
                  lambda t, slot: stage_softmax(tab_ref[0, t], slot, False),
                  nothing,
                  lambda t, slot: stage_values(tab_ref[0, t], tab_ref[1, t], slot))

    def finish(qi, carry):
        rows = pl.ds(pl.multiple_of(qi * tq, tq), tq)
        o = (acc_st[0:MLA_V, rows] / acc_st[MLA_V:MLA_V + 1, rows]).T
        o_ref[rows, :] = (o * jax.nn.silu(g_ref[rows, :])).astype(o_ref.dtype)
        return carry

    lax.fori_loop(0, nq, finish, 0)


def _pair_table(nq):
    pairs = [(qi, qi - d) for d in range(1, nq) for qi in range(d, nq)] or [(0, 0)]
    return jnp.asarray(pairs, jnp.int32).T


def _mla_attn(qt, k, vt, h1):
    s = k.shape[0]
    tq = min(MLA_TQ, s)
    tc = min(ATT_TC, tq)
    nq = s // tq
    ga_blk = H1_GA // LANE
    once = pl.Buffered(1)
    return pl.pallas_call(
        functools.partial(_mla_attn_kernel, tq=tq, tc=tc, nq=nq),
        grid=(MLA_HEADS,),
        in_specs=[pl.BlockSpec(memory_space=pltpu.SMEM),
                  pl.BlockSpec((MLA_QK_PAD, s), lambda h: (h, 0)),
                  pl.BlockSpec((s, MLA_QK_PAD), lambda h: (0, h)),
                  pl.BlockSpec((None, MLA_V_AUG, s), lambda h: (h, 0, 0)),
                  pl.BlockSpec((s, LANE), lambda h: (0, ga_blk + h), pipeline_mode=once)],
        out_specs=pl.BlockSpec((s, MLA_V), lambda h: (0, h), pipeline_mode=once),
        out_shape=jax.ShapeDtypeStruct((s, MLA_WIDTH), BF16),
        scratch_shapes=[pltpu.VMEM((1, s), F32), pltpu.VMEM((MLA_V_AUG, s), F32),
                        pltpu.VMEM((2, tq, tq), F32), pltpu.VMEM((2, tq, tq), BF16), pltpu.VMEM((2, 1, tq), F32)],
        compiler_params=_cparams("parallel", flags=ATT_FLAGS),
        name="mla_attn",
    )(_pair_table(nq), qt, k, vt, h1)


def _sb_attn_kernel(tab_ref, qt_ref, k_ref, vt_ref, tri_ref, g_ref, o_ref, r_st, acc_st,
                    lb_buf, hi_buf, x_buf, cs_buf, a_buf, *, tq, tc, nq):
    n_sub = tq // tc
    n_off = nq * (nq - 1) // 2
    r_st[...] = jnp.zeros(r_st.shape, F32)
    acc_st[...] = jnp.zeros(acc_st.shape, F32)
    sub = [slice(t * tc, (t + 1) * tc) for t in range(n_sub)]
    strict = (lax.broadcasted_iota(jnp.int32, (tc, tc), 0)
              < lax.broadcasted_iota(jnp.int32, (tc, tc), 1))

    def q_cols(qi, c):
        return pl.ds(pl.multiple_of(qi * tq + c * tc, tc), tc)

    def stage_a(qi, j, slot, diag):
        for kk in range(n_sub):
            k = k_ref[pl.ds(pl.multiple_of(j * tq + kk * tc, tc), tc), :]
            for c in range(n_sub):
                if diag and kk > c:
                    hi_buf[slot, sub[kk], sub[c]] = jnp.zeros((tc, tc), BF16)
                    continue
                y = jnp.dot(k, qt_ref[:, q_cols(qi, c)], preferred_element_type=F32)
                t = jnp.log(1.0 + jnp.exp2(-jnp.abs(y))) * LOG2E
                neg_part = jnp.minimum(y, 0.0)
                log_1mb = (neg_part - y) - t
                if diag and kk == c:
                    log_1mb = jnp.where(strict, log_1mb, 0.0)
                lb_buf[slot, sub[kk], sub[c]] = neg_part - t
                hi_buf[slot, sub[kk], sub[c]] = log_1mb.astype(BF16)

    def stage_b(slot):
        tri = tri_ref[...]
        for kk in range(n_sub):
            for c in range(n_sub):
                r = jnp.dot(tri, hi_buf[slot, sub[kk], sub[c]], preferred_element_type=F32)
                x_buf[slot, sub[kk], sub[c]] = r[:tc] + lb_buf[slot, sub[kk], sub[c]]
                cs_buf[slot, kk, :, sub[c]] = r[tc:tc + 1]

    def stage_c_weights(qi, slot, diag):
        for c in range(n_sub):
            run = r_st[:, q_cols(qi, c)]
            for kk in range(n_sub - 1, -1, -1):
                if diag and kk > c:
                    a_buf[sub[kk], sub[c]] = jnp.zeros((tc, tc), BF16)
                    continue
                a = jnp.exp2(x_buf[slot, sub[kk], sub[c]] + run)
                if diag and kk == c:
                    a = jnp.where(strict, a, 0.0)
                a_buf[sub[kk], sub[c]] = a.astype(BF16)
                run = run + cs_buf[slot, kk, :, sub[c]]
            r_st[:, q_cols(qi, c)] = run

    def stage_c_values(qi, j):
        vt = vt_ref[:, pl.ds(pl.multiple_of(j * tq, tq), tq)]
        for c in range(n_sub):
            acc_st[:, q_cols(qi, c)] += jnp.dot(vt, a_buf[:, sub[c]], preferred_element_type=F32)

    _run_pipeline(nq,
                  lambda t, slot: stage_a(t, t, slot, True),
                  lambda t, slot: stage_b(slot),
                  lambda t, slot: stage_c_weights(t, slot, True),
                  lambda t, slot: stage_c_values(t, t))
    _run_pipeline(n_off,
                  lambda t, slot: stage_a(tab_ref[0, t], tab_ref[1, t], slot, False),
                  lambda t, slot: stage_b(slot),
                  lambda t, slot: stage_c_weights(tab_ref[0, t], slot, False),
                  lambda t, slot: stage_c_values(tab_ref[0, t], tab_ref[1, t]))

    def finish(qi, carry):
        rows = pl.ds(pl.multiple_of(qi * tq, tq), tq)
        o_ref[rows, :] = (acc_st[:, rows].T * jax.nn.silu(g_ref[rows, :])).astype(o_ref.dtype)
        return carry

    lax.fori_loop(0, nq, finish, 0)


def _sb_attn(h2, h2t, h1, tri):
    s = h2.shape[0]
    tq = min(SB_TQ, s)
    tc = tri.shape[1]
    nq = s // tq
    qb, kb, vb, gb = H2T_SBQ // LANE, H2_SBK // LANE, H2T_SBV // LANE, H1_GC // LANE
    return pl.pallas_call(
        functools.partial(_sb_attn_kernel, tq=tq, tc=tc, nq=nq),
        grid=(SB_HEADS,),
        in_specs=[pl.BlockSpec(memory_space=pltpu.SMEM),
                  pl.BlockSpec((LANE, s), lambda h: (qb + h, 0)),
                  pl.BlockSpec((s, LANE), lambda h: (0, kb + h)),
                  pl.BlockSpec((LANE, s), lambda h: (vb + h, 0)),
                  pl.BlockSpec(tri.shape, lambda h: (0, 0)),
                  pl.BlockSpec((s, LANE), lambda h: (0, gb + h))],
        out_specs=pl.BlockSpec((s, LANE), lambda h: (0, h)),
        out_shape=jax.ShapeDtypeStruct((s, SB_WIDTH), BF16),
        scratch_shapes=[pltpu.VMEM((1, s), F32), pltpu.VMEM((SB_HEAD_DIM, s), F32),
                        pltpu.VMEM((2, tq, tq), F32), pltpu.VMEM((2, tq, tq), BF16),
                        pltpu.VMEM((2, tq, tq), F32), pltpu.VMEM((2, tq // tc, 1, tq), F32),
                        pltpu.VMEM((tq, tq), BF16)],
        compiler_params=_cparams("parallel", flags=ATT_FLAGS),
        name="sb_attn",
    )(_pair_table(nq), h2t, h2, h2t, tri, h1)


def _gmlp_kernel(u_ref, v_ref, g_ref, lng_ref, lnb_ref, w_ref, b_ref, o_ref, *, n_chunks):
    gv = jax.nn.gelu(v_ref[...])
    mu = jnp.mean(gv, axis=-1, keepdims=True)
    vc = gv - mu
    var = jnp.mean(vc * vc, axis=-1, keepdims=True)
    vn = (vc * lax.rsqrt(var + LN_EPS) * lng_ref[...] + lnb_ref[...]).astype(BF16)
    t_chunk = lax.broadcasted_iota(jnp.int32, (SG_CHUNK, SG_CHUNK), 0) // CHUNK
    s_chunk = lax.broadcasted_iota(jnp.int32, (SG_CHUNK, SG_CHUNK), 1) // CHUNK
    mask = (s_chunk <= t_chunk).astype(F32)
    for g in range(SG_GROUPS):
        w_sp = (w_ref[g] * mask).astype(BF16)
        bias = b_ref[:, g:g + 1]
        cols = slice(g * SG_GROUP_CH, (g + 1) * SG_GROUP_CH)
        for n in range(n_chunks):
            rows = slice(n * SG_CHUNK, (n + 1) * SG_CHUNK)
            mixed = jnp.dot(w_sp, vn[rows, cols], preferred_element_type=F32) + bias
            o_b = jax.nn.gelu(u_ref[rows, cols]) * mixed
            o_ref[rows, cols] = (o_b * jax.nn.silu(g_ref[rows, cols])).astype(o_ref.dtype)


def _gmlp(h1, layer, lng, lnb, w, b_t):
    s = h1.shape[0]
    tm = min(s, 512)
    full = lambda a: pl.BlockSpec((None,) + a.shape[1:], lambda i: (layer,) + (0,) * (a.ndim - 1))
    return pl.pallas_call(
        functools.partial(_gmlp_kernel, n_chunks=tm // SG_CHUNK),
        grid=(s // tm,),
        in_specs=[pl.BlockSpec((tm, SG_WIDTH), lambda i: (i, H1_SGU // SG_WIDTH)),
                  pl.BlockSpec((tm, SG_WIDTH), lambda i: (i, H1_SGV // SG_WIDTH)),
                  pl.BlockSpec((tm, SG_WIDTH), lambda i: (i, H1_GB // SG_WIDTH)),
                  full(lng), full(lnb), full(w), full(b_t)],
        out_specs=pl.BlockSpec((tm, SG_WIDTH), lambda i: (i, 0)),
        out_shape=jax.ShapeDtypeStruct((s, SG_WIDTH), BF16),
        compiler_params=_cparams("parallel"),
        name="gmlp",
    )(h1, h1, h1, lng, lnb, w, b_t)


def _mem_kv_kernel(mem_ref, wk_ref, wv_ref, kbd_ref, vbd_ref):
    mem = mem_ref[...].astype(BF16)
    mk_t = jnp.dot(mem, wk_ref[...], preferred_element_type=F32).T
    mv = jnp.dot(mem, wv_ref[...], preferred_element_type=F32)
    feat_row = lax.broadcasted_iota(jnp.int32, mk_t.shape, 0) // MEM_HEAD_DIM
    feat_col = lax.broadcasted_iota(jnp.int32, mv.shape, 1) // MEM_HEAD_DIM
    for h in range(MEM_HEADS):
        seg = slice(h * MEM_TOKENS, (h + 1) * MEM_TOKENS)
        kbd_ref[:, seg] = jnp.where(feat_row == h, mk_t, 0.0).astype(BF16)
        vbd_ref[seg, :] = jnp.where(feat_col == h, mv, 0.0).astype(BF16)


def _mem_kv(mem, layer, wk, wv):
    nt = MEM_HEADS * MEM_TOKENS
    full2 = lambda shape: pl.BlockSpec(shape, lambda i: (0, 0))
    stack = lambda a: pl.BlockSpec((None,) + a.shape[1:], lambda i: (layer, 0, 0))
    return pl.pallas_call(
        _mem_kv_kernel,
        grid=(1,),
        in_specs=[full2(mem.shape), stack(wk), stack(wv)],
        out_specs=[full2((MEM_WIDTH, nt)), full2((nt, MEM_WIDTH))],
        out_shape=[jax.ShapeDtypeStruct((MEM_WIDTH, nt), BF16),
                   jax.ShapeDtypeStruct((nt, MEM_WIDTH), BF16)],
        compiler_params=_cparams("arbitrary"),
        name="mem_kv",
    )(mem, wk, wv)


def _mem_attn_kernel(q_ref, kbd_ref, vbd_ref, g_ref, o_ref):
    logits = jnp.dot(q_ref[...], kbd_ref[...], preferred_element_type=F32) * MEM_SCALE
    probs = []
    for h in range(MEM_HEADS):
        seg = logits[:, h * MEM_TOKENS:(h + 1) * MEM_TOKENS]
        e = jnp.exp(seg - jnp.max(seg, axis=-1, keepdims=True))
        probs.append((e / jnp.sum(e, axis=-1, keepdims=True)).astype(BF16))
    p = jnp.concatenate(probs, axis=-1)
    o = jnp.dot(p, vbd_ref[...], preferred_element_type=F32)
    o_ref[...] = (o * jax.nn.silu(g_ref[...])).astype(o_ref.dtype)


def _mem_attn(h2, kbd, vbd, h1):
    s = h2.shape[0]
    tm = min(s, 512)
    full2 = lambda shape: pl.BlockSpec(shape, lambda i: (0, 0))
    return pl.pallas_call(
        _mem_attn_kernel,
        grid=(s // tm,),
        in_specs=[pl.BlockSpec((tm, MEM_WIDTH), lambda i: (i, H2_MQ // MEM_WIDTH)),
                  full2(kbd.shape), full2(vbd.shape),
                  pl.BlockSpec((tm, MEM_WIDTH), lambda i: (i, H1_GM // MEM_WIDTH))],
        out_specs=pl.BlockSpec((tm, MEM_WIDTH), lambda i: (i, 0)),
        out_shape=jax.ShapeDtypeStruct((s, MEM_WIDTH), BF16),
        compiler_params=_cparams("parallel"),
        name="mem_attn",
    )(h2, kbd, vbd, h1)


def _out_ln_kernel(ya_ref, yb_ref, yc_ref, ym_ref, w_ref, x_ref, g_ref, b_ref, o_ref, obf_ref, *, alpha):
    y = None
    row = 0
    for y_ref in (ya_ref, yb_ref, yc_ref, ym_ref):
        width = y_ref.shape[1]
        part = jnp.dot(y_ref[...], w_ref[row:row + width, :], preferred_element_type=F32)
        y = part if y is None else y + part
        row += width
    r = alpha * x_ref[...] + y
    mu = jnp.mean(r, axis=-1, keepdims=True)
    rc = r - mu
    var = jnp.mean(rc * rc, axis=-1, keepdims=True)
    out = rc * lax.rsqrt(var + LN_EPS) * g_ref[...] + b_ref[...]
    o_ref[...] = out
    obf_ref[...] = out.astype(BF16)


def _out_ln(ya, yb, yc, ym, w, layer, x, g, b, alpha):
    s, d = x.shape
    tm = min(s, 512)
    row = lambda w: pl.BlockSpec((tm, w), lambda i: (i, 0))
    full = lambda a: pl.BlockSpec((None,) + a.shape[1:], lambda i: (layer, 0, 0))
    return pl.pallas_call(
        functools.partial(_out_ln_kernel, alpha=alpha),
        grid=(s // tm,),
        in_specs=[row(MLA_WIDTH), row(SG_WIDTH), row(SB_WIDTH), row(MEM_WIDTH),
                  full(w), row(d), full(g), full(b)],
        out_specs=[row(d), row(d)],
        out_shape=[jax.ShapeDtypeStruct((s, d), F32), jax.ShapeDtypeStruct((s, d), BF16)],
        compiler_params=_cparams("parallel"),
        name="out_ln",
    )(ya, yb, yc, ym, w, x, g, b)


def _prep_mla_weights(w_uq, w_ukv):
    depth = w_uq.shape[0]
    wq = w_uq.reshape(depth, MLA_Q_RANK, MLA_HEADS, MLA_NOPE + MLA_ROPE)
    wq = jnp.pad(wq, ((0, 0), (0, 0), (0, 0), (0, MLA_QK_PAD - MLA_NOPE - MLA_ROPE)))
    wqt = jnp.swapaxes(wq.reshape(depth, MLA_Q_RANK, MLA_HEADS * MLA_QK_PAD), 1, 2).astype(BF16)
    wkv = w_ukv.reshape(depth, MLA_KV_RANK, MLA_HEADS, MLA_NOPE + MLA_V)
    wk = wkv[..., :MLA_NOPE].reshape(depth, MLA_KV_RANK, MLA_WIDTH).astype(BF16)
    wvt = jnp.swapaxes(wkv[..., MLA_NOPE:].reshape(depth, MLA_KV_RANK, MLA_WIDTH), 1, 2).astype(BF16)
    return wqt, wk, wvt


def kernel(x, mem, positions, w_in, q_norm_g, w_uq, kv_norm_g, w_ukv, sg_ln_g, sg_ln_b, sg_w, sg_b,
           w_mem_k, w_mem_v, w_out, ln_g, ln_b):
    b, s, d = x.shape
    depth = w_in.shape[0]
    alpha = (2.0 * depth) ** 0.25
    sb_tc = min(s, ATT_TC)

    inv_freq = ROPE_THETA ** (-jnp.arange(0, MLA_ROPE, 2, dtype=F32) / MLA_ROPE)
    invf = jnp.concatenate([inv_freq, inv_freq, jnp.zeros((LANE - MLA_ROPE,), F32)])
    tri = jnp.concatenate(
        [(lax.broadcasted_iota(jnp.int32, (sb_tc, sb_tc), 1)
          > lax.broadcasted_iota(jnp.int32, (sb_tc, sb_tc), 0)).astype(BF16),
         jnp.ones((BF16_ROWS, sb_tc), BF16)], axis=0)
    w1, w2, w2q = _w_in_prep(jnp.swapaxes(w_in, 1, 2))
    wqt, wk, wvt = _prep_mla_weights(w_uq, w_ukv)
    wmk, wmv, wo = w_mem_k.astype(BF16), w_mem_v.astype(BF16), w_out.astype(BF16)
    row3 = lambda p: p[:, None, :]
    sg_b_t = jnp.swapaxes(sg_b, 1, 2)
    h2t_scale = jnp.concatenate([jnp.full((SB_WIDTH, 1), SB_SCALE * LOG2E, F32),
                                 jnp.ones((SB_WIDTH, 1), F32)], axis=0)

    outs = []
    for bi in range(b):
        xf = x[bi]
        xb = xf.astype(BF16)
        cos, sin, cos_t, sin_t = _rope_tables(positions[bi].reshape(s, 1), positions[bi].reshape(1, s),
                                              invf[None, :], invf[:, None])
        for l in range(depth):
            h1 = _matmul(xb, w1, l, F32, 1024, 1024, "in_proj_f32")
            h2 = _matmul(xb, w2, l, BF16, 1024, H2_W, "in_proj_bf16")
            h2t = _matmul_nt(w2q, l, xb, h2t_scale, BF16, 1024, H2T_W, "in_proj_bf16_t")
            qt, k, vt = _mla_prep(h1, l, row3(q_norm_g), row3(kv_norm_g), wqt, wk, wvt, cos, sin, cos_t, sin_t)
            ya = _mla_attn(qt, k, vt, h1)
            yc = _sb_attn(h2, h2t, h1, tri)
            yb = _gmlp(h1, l, row3(sg_ln_g), row3(sg_ln_b), sg_w, sg_b_t)
            kbd, vbd = _mem_kv(mem[bi], l, wmk, wmv)
            ym = _mem_attn(h2, kbd, vbd, h1)
            xf, xb = _out_ln(ya, yb, yc, ym, wo, l, xf, row3(ln_g), row3(ln_b), alpha)
        outs.append(xf)
    return outs[0][None] if b == 1 else jnp.stack(outs, axis=0)
```

```python
import functools
import math

import jax
import jax.numpy as jnp
from jax import lax
from jax.experimental import pallas as pl
from jax.experimental.pallas import tpu as pltpu

F32 = jnp.float32
BF16 = jnp.bfloat16

D_MODEL = 2048
CHUNK = 64
MLA_HEADS = 6
MLA_NOPE = 128
MLA_ROPE = 64
MLA_V = 128
MLA_Q_RANK = 512
MLA_KV_RANK = 256
MLA_WIDTH = MLA_HEADS * MLA_V
ROPE_THETA = 10000.0
SG_GROUPS = 4
SG_GROUP_CH = 128
SG_WIDTH = SG_GROUPS * SG_GROUP_CH
SG_CHUNK = 128
SB_HEADS = 4
SB_HEAD_DIM = 128
SB_WIDTH = SB_HEADS * SB_HEAD_DIM
MEM_TOKENS = 256
MEM_HEADS = 4
MEM_HEAD_DIM = 64
MEM_WIDTH = MEM_HEADS * MEM_HEAD_DIM
LN_EPS = 1e-5
RMS_EPS = 1e-6

LOG2E = math.log2(math.e)
MLA_SCALE = 1.0 / math.sqrt(MLA_NOPE + MLA_ROPE)
SB_SCALE = 1.0 / math.sqrt(SB_HEAD_DIM)
MEM_SCALE = 1.0 / math.sqrt(MEM_HEAD_DIM)

LANE = 128
BF16_ROWS = 16
MXU_TILE = 256
MLA_QK_PAD = 256
MLA_V_AUG = MLA_V + BF16_ROWS
NEG_BIG = -1e30
VMEM_LIMIT = 48 * 1024 * 1024
MLA_TQ = 1024
SB_TQ = 512
ATT_TC = MXU_TILE

H1_LAT_W = 1024
H1_SGU = 1024
H1_SGV = 1536
H1_GB = 2048
H1_GA = 2560
H1_GC = 3328
H1_GM = 3840
H1_W = 4096
H2_SBK = 0
H2_MQ = 512
H2_W = 768
H2T_SBQ = 0
H2T_SBV = 512
H2T_W = 1024

_NT = (((1,), (1,)), ((), ()))


def _cparams(*sem, flags=None):
    return pltpu.CompilerParams(dimension_semantics=sem, vmem_limit_bytes=VMEM_LIMIT, flags=flags)


ATT_FLAGS = None


def _rope_table_kernel(pos_col_ref, pos_row_ref, invf_row_ref, invf_col_ref, cos_ref, sin_ref, cos_t_ref, sin_t_ref):
    ang = pos_col_ref[...].astype(F32) * invf_row_ref[...]
    cos_ref[...] = jnp.cos(ang)
    sin_ref[...] = jnp.sin(ang)
    ang_t = invf_col_ref[...] * pos_row_ref[...].astype(F32)
    cos_t_ref[...] = jnp.cos(ang_t)
    sin_t_ref[...] = jnp.sin(ang_t)


def _rope_tables(pos_col, pos_row, invf_row, invf_col):
    s = pos_col.shape[0]
    tm = min(s, 1024)
    return pl.pallas_call(
        _rope_table_kernel,
        grid=(s // tm,),
        in_specs=[pl.BlockSpec((tm, 1), lambda i: (i, 0)),
                  pl.BlockSpec((1, tm), lambda i: (0, i)),
                  pl.BlockSpec((1, LANE), lambda i: (0, 0)),
                  pl.BlockSpec((LANE, 1), lambda i: (0, 0))],
        out_specs=[pl.BlockSpec((tm, LANE), lambda i: (i, 0)),
                   pl.BlockSpec((tm, LANE), lambda i: (i, 0)),
                   pl.BlockSpec((LANE, tm), lambda i: (0, i)),
                   pl.BlockSpec((LANE, tm), lambda i: (0, i))],
        out_shape=[jax.ShapeDtypeStruct((s, LANE), F32)] * 2 + [jax.ShapeDtypeStruct((LANE, s), F32)] * 2,
        compiler_params=_cparams("parallel"),
        name="rope_tables",
    )(pos_col, pos_row, invf_row, invf_col)


_IN_SEGMENTS = ("c_q", "c_kv", "k_pe", "g_a", "sg_u", "sg_v", "g_b", "sb_q", "sb_k", "sb_v", "g_c", "m_q", "g_m")
_IN_WIDTHS = (MLA_Q_RANK, MLA_KV_RANK, MLA_ROPE, MLA_WIDTH, SG_WIDTH, SG_WIDTH, SG_WIDTH,
              SB_WIDTH, SB_WIDTH, SB_WIDTH, SB_WIDTH, MEM_WIDTH, MEM_WIDTH)
_IN_OFFSETS = tuple(sum(_IN_WIDTHS[:i]) for i in range(len(_IN_WIDTHS) + 1))
IN_ROWS = {name: (_IN_OFFSETS[i], _IN_OFFSETS[i + 1]) for i, name in enumerate(_IN_SEGMENTS)}


def _w_in_prep_kernel(wt_ref, w1_ref, w2_ref, w2q_ref):
    def put(dst, off, name):
        a, b = IN_ROWS[name]
        dst[off:off + b - a, :] = wt_ref[a:b, :].astype(BF16)

    lat_end = MLA_Q_RANK + MLA_KV_RANK + MLA_ROPE
    for name, off in (("c_q", 0), ("c_kv", MLA_Q_RANK), ("k_pe", MLA_Q_RANK + MLA_KV_RANK),
                      ("sg_u", H1_SGU), ("sg_v", H1_SGV), ("g_b", H1_GB), ("g_a", H1_GA), ("g_c", H1_GC),
                      ("g_m", H1_GM)):
        put(w1_ref, off, name)
    w1_ref[lat_end:H1_LAT_W, :] = jnp.zeros((H1_LAT_W - lat_end, w1_ref.shape[1]), BF16)
    put(w2_ref, H2_SBK, "sb_k")
    put(w2_ref, H2_MQ, "m_q")
    put(w2q_ref, H2T_SBQ, "sb_q")
    put(w2q_ref, H2T_SBV, "sb_v")


def _w_in_prep(w_in_t):
    depth, n, k = w_in_t.shape
    tk = 256
    blk = lambda rows: pl.BlockSpec((None, rows, tk), lambda l, i: (l, 0, i))
    return pl.pallas_call(
        _w_in_prep_kernel,
        grid=(depth, k // tk),
        in_specs=[blk(n)],
        out_specs=[blk(H1_W), blk(H2_W), blk(H2T_W)],
        out_shape=[jax.ShapeDtypeStruct((depth, H1_W, k), BF16),
                   jax.ShapeDtypeStruct((depth, H2_W, k), BF16),
                   jax.ShapeDtypeStruct((depth, H2T_W, k), BF16)],
        compiler_params=_cparams("parallel", "parallel"),
        name="w_in_prep",
    )(w_in_t)


def _mm_kernel(x_ref, wt_ref, o_ref):
    o_ref[...] = lax.dot_general(x_ref[...], wt_ref[...], _NT, preferred_element_type=F32).astype(o_ref.dtype)


def _matmul(x, wt, layer, out_dtype, tm, tn, name):
    m, k = x.shape
    n = wt.shape[1]
    tm = min(tm, m)
    return pl.pallas_call(
        _mm_kernel,
        grid=(m // tm, n // tn),
        in_specs=[pl.BlockSpec((tm, k), lambda i, j: (i, 0)),
                  pl.BlockSpec((None, tn, k), lambda i, j: (layer, j, 0))],
        out_specs=pl.BlockSpec((tm, tn), lambda i, j: (i, j)),
        out_shape=jax.ShapeDtypeStruct((m, n), out_dtype),
        compiler_params=_cparams("parallel", "parallel"),
        name=name,
    )(x, wt)


def _mm_nt_kernel(wt_ref, x_ref, scale_ref, o_ref):
    acc = lax.dot_general(wt_ref[...], x_ref[...], _NT, preferred_element_type=F32)
    o_ref[...] = (acc * scale_ref[...]).astype(o_ref.dtype)


def _matmul_nt(wt, layer, x, row_scale, out_dtype, tm, tn, name):
    n, k = wt.shape[1:]
    m = x.shape[0]
    tm = min(tm, m)
    return pl.pallas_call(
        _mm_nt_kernel,
        grid=(m // tm, n // tn),
        in_specs=[pl.BlockSpec((None, tn, k), lambda i, j: (layer, j, 0)),
                  pl.BlockSpec((tm, k), lambda i, j: (i, 0)),
                  pl.BlockSpec((tn, 1), lambda i, j: (j, 0))],
        out_specs=pl.BlockSpec((tn, tm), lambda i, j: (j, i)),
        out_shape=jax.ShapeDtypeStruct((n, m), out_dtype),
        compiler_params=_cparams("parallel", "parallel"),
        name=name,
    )(wt, x, row_scale)


def _rms(x, g):
    ms = jnp.mean(x * x, axis=-1, keepdims=True)
    return x * lax.rsqrt(ms + RMS_EPS) * g


def _mla_prep_kernel(lat_ref, qg_ref, kvg_ref, wqt_ref, wk_ref, wvt_ref, cos_ref, sin_ref, cos_t_ref, sin_t_ref,
                     qt_ref, k_ref, vt_ref):
    half = MLA_ROPE // 2
    cqn = _rms(lat_ref[:, 0:MLA_Q_RANK], qg_ref[...]).astype(BF16)
    ckvn = _rms(lat_ref[:, MLA_Q_RANK:MLA_Q_RANK + MLA_KV_RANK], kvg_ref[...]).astype(BF16)
    qa_t = lax.dot_general(wqt_ref[...], cqn, _NT, preferred_element_type=F32) * (MLA_SCALE * LOG2E)
    kn = jnp.dot(ckvn, wk_ref[...], preferred_element_type=F32)
    v_t = lax.dot_general(wvt_ref[...], ckvn, _NT, preferred_element_type=F32).astype(BF16)
    ones = jnp.ones((BF16_ROWS, v_t.shape[1]), BF16)
    cos_t = cos_t_ref[...]
    sin_t = sin_t_ref[...]
    kpe = lat_ref[:, MLA_Q_RANK + MLA_KV_RANK:MLA_Q_RANK + MLA_KV_RANK + LANE]
    lane = lax.broadcasted_iota(jnp.int32, kpe.shape, 1)
    kpe_rot = jnp.where(lane < half, -pltpu.roll(kpe, LANE - half, 1),
                        jnp.where(lane < MLA_ROPE, pltpu.roll(kpe, half, 1), 0.0))
    krot = (kpe * cos_ref[...] + kpe_rot * sin_ref[...]).astype(BF16)
    for h in range(MLA_HEADS):
        c0 = h * MLA_QK_PAD
        qt_ref[c0:c0 + LANE, :] = qa_t[c0:c0 + LANE, :].astype(BF16)
        qr = qa_t[c0 + LANE:c0 + 2 * LANE, :]
        qr_rot = jnp.concatenate([-qr[half:MLA_ROPE], qr[0:half], qr[MLA_ROPE:LANE]], axis=0)
        qt_ref[c0 + LANE:c0 + 2 * LANE, :] = (qr * cos_t + qr_rot * sin_t).astype(BF16)
        k_ref[:, c0:c0 + LANE] = kn[:, h * LANE:(h + 1) * LANE].astype(BF16)
        k_ref[:, c0 + LANE:c0 + 2 * LANE] = krot
        vt_ref[h, 0:MLA_V, :] = v_t[h * MLA_V:(h + 1) * MLA_V, :]
        vt_ref[h, MLA_V:MLA_V_AUG, :] = ones


def _mla_prep(h1, layer, qg, kvg, wqt, wk, wvt, cos, sin, cos_t, sin_t):
    s = h1.shape[0]
    tm = min(s, 512)
    nq = MLA_HEADS * MLA_QK_PAD
    full = lambda a: pl.BlockSpec((None,) + a.shape[1:], lambda i: (layer, 0, 0))
    rows = lambda w: pl.BlockSpec((tm, w), lambda i: (i, 0))
    cols = lambda w: pl.BlockSpec((w, tm), lambda i: (0, i))
    return pl.pallas_call(
        _mla_prep_kernel,
        grid=(s // tm,),
        in_specs=[rows(H1_LAT_W), full(qg), full(kvg), full(wqt), full(wk), full(wvt),
                  rows(LANE), rows(LANE), cols(LANE), cols(LANE)],
        out_specs=[cols(nq), rows(nq), pl.BlockSpec((MLA_HEADS, MLA_V_AUG, tm), lambda i: (0, 0, i))],
        out_shape=[jax.ShapeDtypeStruct((nq, s), BF16),
                   jax.ShapeDtypeStruct((s, nq), BF16),
                   jax.ShapeDtypeStruct((MLA_HEADS, MLA_V_AUG, s), BF16)],
        compiler_params=_cparams("parallel"),
        name="mla_prep",
    )(h1, qg, kvg, wqt, wk, wvt, cos, sin, cos_t, sin_t)


def _run_pipeline(n_items, stage_a, stage_b, stage_c_weights, stage_c_values):
    if n_items == 0:
        return

    def stage_c(t, slot):
        stage_c_weights(t, slot)
        stage_c_values(t, slot)

    def trip(i, slot):
        stage_c_weights(i - 2, slot)
        stage_a(i, slot)
        stage_c_values(i - 2, slot)
        stage_b(i - 1, 1 - slot)

    stage_a(0, 0)
    if n_items >= 2:
        stage_a(1, 1)
    stage_b(0, 0)
    steady = max(n_items - 2, 0)

    def body(ii, carry):
        i = 2 + 2 * ii
        trip(i, 0)
        trip(i + 1, 1)
        return carry

    lax.fori_loop(0, steady // 2, body, 0)
    if steady % 2:
        trip(n_items - 1, (n_items - 1) % 2)
    if n_items >= 2:
        stage_c_weights(n_items - 2, n_items % 2)
        stage_b(n_items - 1, (n_items - 1) % 2)
        stage_c_values(n_items - 2, n_items % 2)
    stage_c(n_items - 1, (n_items - 1) % 2)


def _mla_attn_kernel(qt_ref, k_ref, vt_ref, g_ref, o_ref, m_sc, acc_sc, s_buf, p_buf, al_buf, *, tq, tc):
    n = pl.program_id(1) + 1
    m_sc[...] = jnp.full(m_sc.shape, NEG_BIG, F32)
    acc_sc[...] = jnp.zeros(acc_sc.shape, F32)
    col_tiles = [slice(c * tc, (c + 1) * tc) for c in range(tq // tc)]

    def stage_scores(j, slot):
        k = k_ref[pl.ds(pl.multiple_of(j * tq, tq), tq), :]
        for cols in col_tiles:
            s_buf[slot, :, cols] = jnp.dot(k, qt_ref[:, cols], preferred_element_type=F32)

    def stage_softmax(slot, masked):
        for c, cols in enumerate(col_tiles):
            n_keys = (c + 1) * tc if masked else tq
            s = s_buf[slot, 0:n_keys, cols]
            if masked:
                kc = lax.broadcasted_iota(jnp.int32, s.shape, 0) // CHUNK
                qc = (lax.broadcasted_iota(jnp.int32, s.shape, 1) + c * tc) // CHUNK
                s = jnp.where(kc <= qc, s, NEG_BIG)
            m_prev = m_sc[:, cols]
            m_new = jnp.maximum(m_prev, jnp.max(s, axis=0, keepdims=True))
            p_buf[slot, 0:n_keys, cols] = jnp.exp2(s - m_new).astype(BF16)
            if n_keys < tq:
                p_buf[slot, n_keys:tq, cols] = jnp.zeros((tq - n_keys, tc), BF16)
            al_buf[slot, :, cols] = jnp.exp2(m_prev - m_new)
            m_sc[:, cols] = m_new

    def stage_values(j, slot):
        vt = vt_ref[:, pl.ds(pl.multiple_of(j * tq, tq), tq)]
        for cols in col_tiles:
            pv = jnp.dot(vt, p_buf[slot, :, cols], preferred_element_type=F32)
            acc_sc[:, cols] = al_buf[slot, :, cols] * acc_sc[:, cols] + pv

    def trip(i, slot):
        stage_scores(i, slot)
        stage_values(i - 2, slot)
        stage_softmax(1 - slot, False)

    def drain(last_slot):
        stage_values(n - 2, 1 - last_slot)
        stage_softmax(last_slot, True)
        stage_values(n - 1, last_slot)

    stage_scores(0, 0)

    @pl.when(n >= 2)
    def _():
        stage_softmax(0, False)
        stage_scores(1, 1)

    def body(ii, carry):
        i = 2 + 2 * ii
        trip(i, 0)
        trip(i + 1, 1)
        return carry

    lax.fori_loop(0, (n - 2) // 2, body, 0)
    n_odd = n % 2 == 1

    @pl.when(jnp.logical_and(n_odd, n >= 3))
    def _():
        trip(n - 1, 0)
        drain(0)

    @pl.when(jnp.logical_not(n_odd))
    def _():
        drain(1)

    @pl.when(n == 1)
    def _():
        stage_softmax(0, True)
        stage_values(0, 0)

    o = (acc_sc[0:MLA_V, :] / acc_sc[MLA_V:MLA_V + 1, :]).T
    o_ref[...] = (o * jax.nn.silu(g_ref[...])).astype(o_ref.dtype)


def _pair_table(nq):
    pairs = [(qi, qi - d) for d in range(1, nq) for qi in range(d, nq)] or [(0, 0)]
    return jnp.asarray(pairs, jnp.int32).T


def _mla_attn(qt, k, vt, h1):
    s = k.shape[0]
    tq = min(MLA_TQ, s)
    tc = min(ATT_TC, tq)
    ga_blk = H1_GA // LANE
    return pl.pallas_call(
        functools.partial(_mla_attn_kernel, tq=tq, tc=tc),
        grid=(MLA_HEADS, s // tq),
        in_specs=[pl.BlockSpec((MLA_QK_PAD, tq), lambda h, i: (h, i)),
                  pl.BlockSpec((s, MLA_QK_PAD), lambda h, i: (0, h)),
                  pl.BlockSpec((None, MLA_V_AUG, s), lambda h, i: (h, 0, 0)),
                  pl.BlockSpec((tq, LANE), lambda h, i: (i, ga_blk + h))],
        out_specs=pl.BlockSpec((tq, MLA_V), lambda h, i: (i, h)),
        out_shape=jax.ShapeDtypeStruct((s, MLA_WIDTH), BF16),
        scratch_shapes=[pltpu.VMEM((1, tq), F32), pltpu.VMEM((MLA_V_AUG, tq), F32),
                        pltpu.VMEM((2, tq, tq), F32), pltpu.VMEM((2, tq, tq), BF16), pltpu.VMEM((2, 1, tq), F32)],
        compiler_params=_cparams("parallel", "parallel", flags=ATT_FLAGS),
        name="mla_attn",
    )(qt, k, vt, h1)


def _sb_attn_kernel(tab_ref, qt_ref, k_ref, vt_ref, tri_ref, g_ref, o_ref, r_st, acc_st,
                    lb_buf, hi_buf, x_buf, cs_buf, a_buf, *, tq, tc, nq):
    n_sub = tq // tc
    n_off = nq * (nq - 1) // 2
    r_st[...] = jnp.zeros(r_st.shape, F32)
    acc_st[...] = jnp.zeros(acc_st.shape, F32)
    sub = [slice(t * tc, (t + 1) * tc) for t in range(n_sub)]
    strict = (lax.broadcasted_iota(jnp.int32, (tc, tc), 0)
              < lax.broadcasted_iota(jnp.int32, (tc, tc), 1))

    def q_cols(qi, c):
        return pl.ds(pl.multiple_of(qi * tq + c * tc, tc), tc)

    def stage_a(qi, j, slot, diag):
        for kk in range(n_sub):
            k = k_ref[pl.ds(pl.multiple_of(j * tq + kk * tc, tc), tc), :]
            for c in range(n_sub):
                if diag and kk > c:
                    hi_buf[slot, sub[kk], sub[c]] = jnp.zeros((tc, tc), BF16)
                    continue
                y = jnp.dot(k, qt_ref[:, q_cols(qi, c)], preferred_element_type=F32)
                t = jnp.log(1.0 + jnp.exp2(-jnp.abs(y))) * LOG2E
                neg_part = jnp.minimum(y, 0.0)
                log_1mb = (neg_part - y) - t
                if diag and kk == c:
                    log_1mb = jnp.where(strict, log_1mb, 0.0)
                lb_buf[slot, sub[kk], sub[c]] = neg_part - t
                hi_buf[slot, sub[kk], sub[c]] = log_1mb.astype(BF16)

    def stage_b(slot):
        tri = tri_ref[...]
        for kk in range(n_sub):
            for c in range(n_sub):
                r = jnp.dot(tri, hi_buf[slot, sub[kk], sub[c]], preferred_element_type=F32)
                x_buf[slot, sub[kk], sub[c]] = r[:tc] + lb_buf[slot, sub[kk], sub[c]]
                cs_buf[slot, kk, :, sub[c]] = r[tc:tc + 1]

    def stage_c_weights(qi, slot, diag):
        for c in range(n_sub):
            run = r_st[:, q_cols(qi, c)]
            for kk in range(n_sub - 1, -1, -1):
                if diag and kk > c:
                    a_buf[sub[kk], sub[c]] = jnp.zeros((tc, tc), BF16)
                    continue
                a = jnp.exp2(x_buf[slot, sub[kk], sub[c]] + run)
                if diag and kk == c:
                    a = jnp.where(strict, a, 0.0)
                a_buf[sub[kk], sub[c]] = a.astype(BF16)
                run = run + cs_buf[slot, kk, :, sub[c]]
            r_st[:, q_cols(qi, c)] = run

    def stage_c_values(qi, j):
        vt = vt_ref[:, pl.ds(pl.multiple_of(j * tq, tq), tq)]
        for c in range(n_sub):
            acc_st[:, q_cols(qi, c)] += jnp.dot(vt, a_buf[:, sub[c]], preferred_element_type=F32)

    _run_pipeline(nq,
                  lambda t, slot: stage_a(t, t, slot, True),
                  lambda t, slot: stage_b(slot),
                  lambda t, slot: stage_c_weights(t, slot, True),
                  lambda t, slot: stage_c_values(t, t))
    _run_pipeline(n_off,
                  lambda t, slot: stage_a(tab_ref[0, t], tab_ref[1, t], slot, False),
                  lambda t, slot: stage_b(slot),
                  lambda t, slot: stage_c_weights(tab_ref[0, t], slot, False),
                  lambda t, slot: stage_c_values(tab_ref[0, t], tab_ref[1, t]))

    def finish(qi, carry):
        rows = pl.ds(pl.multiple_of(qi * tq, tq), tq)
        o_ref[rows, :] = (acc_st[:, rows].T * jax.nn.silu(g_ref[rows, :])).astype(o_ref.dtype)
        return carry

    lax.fori_loop(0, nq, finish, 0)


def _sb_attn(h2, h2t, h1, tri):
    s = h2.shape[0]
    tq = min(SB_TQ, s)
    tc = tri.shape[1]
    nq = s // tq
    qb, kb, vb, gb = H2T_SBQ // LANE, H2_SBK // LANE, H2T_SBV // LANE, H1_GC // LANE
    return pl.pallas_call(
        functools.partial(_sb_attn_kernel, tq=tq, tc=tc, nq=nq),
        grid=(SB_HEADS,),
        in_specs=[pl.BlockSpec(memory_space=pltpu.SMEM),
                  pl.BlockSpec((LANE, s), lambda h: (qb + h, 0)),
                  pl.BlockSpec((s, LANE), lambda h: (0, kb + h)),
                  pl.BlockSpec((LANE, s), lambda h: (vb + h, 0)),
                  pl.BlockSpec(tri.shape, lambda h: (0, 0)),
                  pl.BlockSpec((s, LANE), lambda h: (0, gb + h))],
        out_specs=pl.BlockSpec((s, LANE), lambda h: (0, h)),
        out_shape=jax.ShapeDtypeStruct((s, SB_WIDTH), BF16),
        scratch_shapes=[pltpu.VMEM((1, s), F32), pltpu.VMEM((SB_HEAD_DIM, s), F32),
                        pltpu.VMEM((2, tq, tq), F32), pltpu.VMEM((2, tq, tq), BF16),
                        pltpu.VMEM((2, tq, tq), F32), pltpu.VMEM((2, tq // tc, 1, tq), F32),
                        pltpu.VMEM((tq, tq), BF16)],
        compiler_params=_cparams("parallel", flags=ATT_FLAGS),
        name="sb_attn",
    )(_pair_table(nq), h2t, h2, h2t, tri, h1)


def _gmlp_kernel(u_ref, v_ref, g_ref, lng_ref, lnb_ref, w_ref, b_ref, o_ref, *, n_chunks):
    gv = jax.nn.gelu(v_ref[...])
    mu = jnp.mean(gv, axis=-1, keepdims=True)
    vc = gv - mu
    var = jnp.mean(vc * vc, axis=-1, keepdims=True)
    vn = (vc * lax.rsqrt(var + LN_EPS) * lng_ref[...] + lnb_ref[...]).astype(BF16)
    t_chunk = lax.broadcasted_iota(jnp.int32, (SG_CHUNK, SG_CHUNK), 0) // CHUNK
    s_chunk = lax.broadcasted_iota(jnp.int32, (SG_CHUNK, SG_CHUNK), 1) // CHUNK
    mask = (s_chunk <= t_chunk).astype(F32)
    for g in range(SG_GROUPS):
        w_sp = (w_ref[g] * mask).astype(BF16)
        bias = b_ref[:, g:g + 1]
        cols = slice(g * SG_GROUP_CH, (g + 1) * SG_GROUP_CH)
        for n in range(n_chunks):
            rows = slice(n * SG_CHUNK, (n + 1) * SG_CHUNK)
            mixed = jnp.dot(w_sp, vn[rows, cols], preferred_element_type=F32) + bias
            o_b = jax.nn.gelu(u_ref[rows, cols]) * mixed
            o_ref[rows, cols] = (o_b * jax.nn.silu(g_ref[rows, cols])).astype(o_ref.dtype)


def _gmlp(h1, layer, lng, lnb, w, b_t):
    s = h1.shape[0]
    tm = min(s, 512)
    full = lambda a: pl.BlockSpec((None,) + a.shape[1:], lambda i: (layer,) + (0,) * (a.ndim - 1))
    return pl.pallas_call(
        functools.partial(_gmlp_kernel, n_chunks=tm // SG_CHUNK),
        grid=(s // tm,),
        in_specs=[pl.BlockSpec((tm, SG_WIDTH), lambda i: (i, H1_SGU // SG_WIDTH)),
                  pl.BlockSpec((tm, SG_WIDTH), lambda i: (i, H1_SGV // SG_WIDTH)),
                  pl.BlockSpec((tm, SG_WIDTH), lambda i: (i, H1_GB // SG_WIDTH)),
                  full(lng), full(lnb), full(w), full(b_t)],
        out_specs=pl.BlockSpec((tm, SG_WIDTH), lambda i: (i, 0)),
        out_shape=jax.ShapeDtypeStruct((s, SG_WIDTH), BF16),
        compiler_params=_cparams("parallel"),
        name="gmlp",
    )(h1, h1, h1, lng, lnb, w, b_t)


def _mem_kv_kernel(mem_ref, wk_ref, wv_ref, kbd_ref, vbd_ref):
    mem = mem_ref[...].astype(BF16)
    mk_t = jnp.dot(mem, wk_ref[...], preferred_element_type=F32).T
    mv = jnp.dot(mem, wv_ref[...], preferred_element_type=F32)
    feat_row = lax.broadcasted_iota(jnp.int32, mk_t.shape, 0) // MEM_HEAD_DIM
    feat_col = lax.broadcasted_iota(jnp.int32, mv.shape, 1) // MEM_HEAD_DIM
    for h in range(MEM_HEADS):
        seg = slice(h * MEM_TOKENS, (h + 1) * MEM_TOKENS)
        kbd_ref[:, seg] = jnp.where(feat_row == h, mk_t, 0.0).astype(BF16)
        vbd_ref[seg, :] = jnp.where(feat_col == h, mv, 0.0).astype(BF16)


def _mem_kv(mem, layer, wk, wv):
    nt = MEM_HEADS * MEM_TOKENS
    full2 = lambda shape: pl.BlockSpec(shape, lambda i: (0, 0))
    stack = lambda a: pl.BlockSpec((None,) + a.shape[1:], lambda i: (layer, 0, 0))
    return pl.pallas_call(
        _mem_kv_kernel,
        grid=(1,),
        in_specs=[full2(mem.shape), stack(wk), stack(wv)],
        out_specs=[full2((MEM_WIDTH, nt)), full2((nt, MEM_WIDTH))],
        out_shape=[jax.ShapeDtypeStruct((MEM_WIDTH, nt), BF16),
                   jax.ShapeDtypeStruct((nt, MEM_WIDTH), BF16)],
        compiler_params=_cparams("arbitrary"),
        name="mem_kv",
    )(mem, wk, wv)


def _mem_attn_kernel(q_ref, kbd_ref, vbd_ref, g_ref, o_ref):
    logits = jnp.dot(q_ref[...], kbd_ref[...], preferred_element_type=F32) * MEM_SCALE
    probs = []
    for h in range(MEM_HEADS):
        seg = logits[:, h * MEM_TOKENS:(h + 1) * MEM_TOKENS]
        e = jnp.exp(seg - jnp.max(seg, axis=-1, keepdims=True))
        probs.append((e / jnp.sum(e, axis=-1, keepdims=True)).astype(BF16))
    p = jnp.concatenate(probs, axis=-1)
    o = jnp.dot(p, vbd_ref[...], preferred_element_type=F32)
    o_ref[...] = (o * jax.nn.silu(g_ref[...])).astype(o_ref.dtype)


def _mem_attn(h2, kbd, vbd, h1):
    s = h2.shape[0]
    tm = min(s, 512)
    full2 = lambda shape: pl.BlockSpec(shape, lambda i: (0, 0))
    return pl.pallas_call(
        _mem_attn_kernel,
        grid=(s // tm,),
        in_specs=[pl.BlockSpec((tm, MEM_WIDTH), lambda i: (i, H2_MQ // MEM_WIDTH)),
                  full2(kbd.shape), full2(vbd.shape),
                  pl.BlockSpec((tm, MEM_WIDTH), lambda i: (i, H1_GM // MEM_WIDTH))],
        out_specs=pl.BlockSpec((tm, MEM_WIDTH), lambda i: (i, 0)),
        out_shape=jax.ShapeDtypeStruct((s, MEM_WIDTH), BF16),
        compiler_params=_cparams("parallel"),
        name="mem_attn",
    )(h2, kbd, vbd, h1)


def _out_ln_kernel(ya_ref, yb_ref, yc_ref, ym_ref, w_ref, x_ref, g_ref, b_ref, o_ref, obf_ref, *, alpha):
    y = None
    row = 0
    for y_ref in (ya_ref, yb_ref, yc_ref, ym_ref):
        width = y_ref.shape[1]
        part = jnp.dot(y_ref[...], w_ref[row:row + width, :], preferred_element_type=F32)
        y = part if y is None else y + part
        row += width
    r = alpha * x_ref[...] + y
    mu = jnp.mean(r, axis=-1, keepdims=True)
    rc = r - mu
    var = jnp.mean(rc * rc, axis=-1, keepdims=True)
    out = rc * lax.rsqrt(var + LN_EPS) * g_ref[...] + b_ref[...]
    o_ref[...] = out
    obf_ref[...] = out.astype(BF16)


def _out_ln(ya, yb, yc, ym, w, layer, x, g, b, alpha):
    s, d = x.shape
    tm = min(s, 512)
    row = lambda w: pl.BlockSpec((tm, w), lambda i: (i, 0))
    full = lambda a: pl.BlockSpec((None,) + a.shape[1:], lambda i: (layer, 0, 0))
    return pl.pallas_call(
        functools.partial(_out_ln_kernel, alpha=alpha),
        grid=(s // tm,),
        in_specs=[row(MLA_WIDTH), row(SG_WIDTH), row(SB_WIDTH), row(MEM_WIDTH),
                  full(w), row(d), full(g), full(b)],
        out_specs=[row(d), row(d)],
        out_shape=[jax.ShapeDtypeStruct((s, d), F32), jax.ShapeDtypeStruct((s, d), BF16)],
        compiler_params=_cparams("parallel"),
        name="out_ln",
    )(ya, yb, yc, ym, w, x, g, b)


def _prep_mla_weights(w_uq, w_ukv):
    depth = w_uq.shape[0]
    wq = w_uq.reshape(depth, MLA_Q_RANK, MLA_HEADS, MLA_NOPE + MLA_ROPE)
    wq = jnp.pad(wq, ((0, 0), (0, 0), (0, 0), (0, MLA_QK_PAD - MLA_NOPE - MLA_ROPE)))
    wqt = jnp.swapaxes(wq.reshape(depth, MLA_Q_RANK, MLA_HEADS * MLA_QK_PAD), 1, 2).astype(BF16)
    wkv = w_ukv.reshape(depth, MLA_KV_RANK, MLA_HEADS, MLA_NOPE + MLA_V)
    wk = wkv[..., :MLA_NOPE].reshape(depth, MLA_KV_RANK, MLA_WIDTH).astype(BF16)
    wvt = jnp.swapaxes(wkv[..., MLA_NOPE:].reshape(depth, MLA_KV_RANK, MLA_WIDTH), 1, 2).astype(BF16)
    return wqt, wk, wvt


def kernel(x, mem, positions, w_in, q_norm_g, w_uq, kv_norm_g, w_ukv, sg_ln_g, sg_ln_b, sg_w, sg_b,
           w_mem_k, w_mem_v, w_out, ln_g, ln_b):
    b, s, d = x.shape
    depth = w_in.shape[0]
    alpha = (2.0 * depth) ** 0.25
    sb_tc = min(s, ATT_TC)

    inv_freq = ROPE_THETA ** (-jnp.arange(0, MLA_ROPE, 2, dtype=F32) / MLA_ROPE)
    invf = jnp.concatenate([inv_freq, inv_freq, jnp.zeros((LANE - MLA_ROPE,), F32)])
    tri = jnp.concatenate(
        [(lax.broadcasted_iota(jnp.int32, (sb_tc, sb_tc), 1)
          > lax.broadcasted_iota(jnp.int32, (sb_tc, sb_tc), 0)).astype(BF16),
         jnp.ones((BF16_ROWS, sb_tc), BF16)], axis=0)
    w1, w2, w2q = _w_in_prep(jnp.swapaxes(w_in, 1, 2))
    wqt, wk, wvt = _prep_mla_weights(w_uq, w_ukv)
    wmk, wmv, wo = w_mem_k.astype(BF16), w_mem_v.astype(BF16), w_out.astype(BF16)
    row3 = lambda p: p[:, None, :]
    sg_b_t = jnp.swapaxes(sg_b, 1, 2)
    h2t_scale = jnp.concatenate([jnp.full((SB_WIDTH, 1), SB_SCALE * LOG2E, F32),
                                 jnp.ones((SB_WIDTH, 1), F32)], axis=0)

    outs = []
    for bi in range(b):
        xf = x[bi]
        xb = xf.astype(BF16)
        cos, sin, cos_t, sin_t = _rope_tables(positions[bi].reshape(s, 1), positions[bi].reshape(1, s),
                                              invf[None, :], invf[:, None])
        for l in range(depth):
            h1 = _matmul(xb, w1, l, F32, 1024, 1024, "in_proj_f32")
            h2 = _matmul(xb, w2, l, BF16, 1024, H2_W, "in_proj_bf16")
            h2t = _matmul_nt(w2q, l, xb, h2t_scale, BF16, 1024, H2T_W, "in_proj_bf16_t")
            qt, k, vt = _mla_prep(h1, l, row3(q_norm_g), row3(kv_norm_g), wqt, wk, wvt, cos, sin, cos_t, sin_t)
            ya = _mla_attn(qt, k, vt, h1)
            yc = _sb_attn(h2, h2t, h1, tri)
            yb = _gmlp(h1, l, row3(sg_ln_g), row3(sg_ln_b), sg_w, sg_b_t)
            kbd, vbd = _mem_kv(mem[bi], l, wmk, wmv)
            ym = _mem_attn(h2, kbd, vbd, h1)
            xf, xb = _out_ln(ya, yb, yc, ym, wo, l, xf, row3(ln_g), row3(ln_b), alpha)
        outs.append(xf)
    return outs[0][None] if b == 1 else jnp.stack(outs, axis=0)
```

```python
import functools
import math

import jax
import jax.numpy as jnp
from jax import lax
from jax.experimental import pallas as pl
from jax.experimental.pallas import tpu as pltpu

F32 = jnp.float32
BF16 = jnp.bfloat16

D_MODEL = 2048
CHUNK = 64
MLA_HEADS = 6
MLA_NOPE = 128
MLA_ROPE = 64
MLA_V = 128
MLA_Q_RANK = 512
MLA_KV_RANK = 256
MLA_WIDTH = MLA_HEADS * MLA_V
ROPE_THETA = 10000.0
SG_GROUPS = 4
SG_GROUP_CH = 128
SG_WIDTH = SG_GROUPS * SG_GROUP_CH
SG_CHUNK = 128
SB_HEADS = 4
SB_HEAD_DIM = 128
SB_WIDTH = SB_HEADS * SB_HEAD_DIM
MEM_TOKENS = 256
MEM_HEADS = 4
MEM_HEAD_DIM = 64
MEM_WIDTH = MEM_HEADS * MEM_HEAD_DIM
LN_EPS = 1e-5
RMS_EPS = 1e-6

LOG2E = math.log2(math.e)
MLA_SCALE = 1.0 / math.sqrt(MLA_NOPE + MLA_ROPE)
SB_SCALE = 1.0 / math.sqrt(SB_HEAD_DIM)
MEM_SCALE = 1.0 / math.sqrt(MEM_HEAD_DIM)

LANE = 128
BF16_ROWS = 16
MXU_TILE = 256
MLA_QK_PAD = 256
MLA_V_AUG = MLA_V + BF16_ROWS
NEG_BIG = -1e30
VMEM_LIMIT = 48 * 1024 * 1024
MLA_TQ = 1024
SB_TQ = 512
ATT_TC = MXU_TILE

H1_LAT_W = 1024
H1_SGU = 1024
H1_SGV = 1536
H1_GB = 2048
H1_GA = 2560
H1_GC = 3328
H1_GM = 3840
H1_W = 4096
H2_SBK = 0
H2_MQ = 512
H2_W = 768
H2T_SBQ = 0
H2T_SBV = 512
H2T_W = 1024

_NT = (((1,), (1,)), ((), ()))


def _cparams(*sem, flags=None):
    return pltpu.CompilerParams(dimension_semantics=sem, vmem_limit_bytes=VMEM_LIMIT, flags=flags)


ATT_FLAGS = None


def _rope_table_kernel(pos_col_ref, pos_row_ref, invf_row_ref, invf_col_ref, cos_ref, sin_ref, cos_t_ref, sin_t_ref):
    ang = pos_col_ref[...].astype(F32) * invf_row_ref[...]
    cos_ref[...] = jnp.cos(ang)
    sin_ref[...] = jnp.sin(ang)
    ang_t = invf_col_ref[...] * pos_row_ref[...].astype(F32)
    cos_t_ref[...] = jnp.cos(ang_t)
    sin_t_ref[...] = jnp.sin(ang_t)


def _rope_tables(pos_col, pos_row, invf_row, invf_col):
    s = pos_col.shape[0]
    tm = min(s, 1024)
    return pl.pallas_call(
        _rope_table_kernel,
        grid=(s // tm,),
        in_specs=[pl.BlockSpec((tm, 1), lambda i: (i, 0)),
                  pl.BlockSpec((1, tm), lambda i: (0, i)),
                  pl.BlockSpec((1, LANE), lambda i: (0, 0)),
                  pl.BlockSpec((LANE, 1), lambda i: (0, 0))],
        out_specs=[pl.BlockSpec((tm, LANE), lambda i: (i, 0)),
                   pl.BlockSpec((tm, LANE), lambda i: (i, 0)),
                   pl.BlockSpec((LANE, tm), lambda i: (0, i)),
                   pl.BlockSpec((LANE, tm), lambda i: (0, i))],
        out_shape=[jax.ShapeDtypeStruct((s, LANE), F32)] * 2 + [jax.ShapeDtypeStruct((LANE, s), F32)] * 2,
        compiler_params=_cparams("parallel"),
        name="rope_tables",
    )(pos_col, pos_row, invf_row, invf_col)


_IN_SEGMENTS = ("c_q", "c_kv", "k_pe", "g_a", "sg_u", "sg_v", "g_b", "sb_q", "sb_k", "sb_v", "g_c", "m_q", "g_m")
_IN_WIDTHS = (MLA_Q_RANK, MLA_KV_RANK, MLA_ROPE, MLA_WIDTH, SG_WIDTH, SG_WIDTH, SG_WIDTH,
              SB_WIDTH, SB_WIDTH, SB_WIDTH, SB_WIDTH, MEM_WIDTH, MEM_WIDTH)
_IN_OFFSETS = tuple(sum(_IN_WIDTHS[:i]) for i in range(len(_IN_WIDTHS) + 1))
IN_ROWS = {name: (_IN_OFFSETS[i], _IN_OFFSETS[i + 1]) for i, name in enumerate(_IN_SEGMENTS)}


def _w_in_prep_kernel(wt_ref, w1_ref, w2_ref, w2q_ref):
    def put(dst, off, name):
        a, b = IN_ROWS[name]
        dst[off:off + b - a, :] = wt_ref[a:b, :].astype(BF16)

    lat_end = MLA_Q_RANK + MLA_KV_RANK + MLA_ROPE
    for name, off in (("c_q", 0), ("c_kv", MLA_Q_RANK), ("k_pe", MLA_Q_RANK + MLA_KV_RANK),
                      ("sg_u", H1_SGU), ("sg_v", H1_SGV), ("g_b", H1_GB), ("g_a", H1_GA), ("g_c", H1_GC),
                      ("g_m", H1_GM)):
        put(w1_ref, off, name)
    w1_ref[lat_end:H1_LAT_W, :] = jnp.zeros((H1_LAT_W - lat_end, w1_ref.shape[1]), BF16)
    put(w2_ref, H2_SBK, "sb_k")
    put(w2_ref, H2_MQ, "m_q")
    put(w2q_ref, H2T_SBQ, "sb_q")
    put(w2q_ref, H2T_SBV, "sb_v")


def _w_in_prep(w_in_t):
    depth, n, k = w_in_t.shape
    tk = 256
    blk = lambda rows: pl.BlockSpec((None, rows, tk), lambda l, i: (l, 0, i))
    return pl.pallas_call(
        _w_in_prep_kernel,
        grid=(depth, k // tk),
        in_specs=[blk(n)],
        out_specs=[blk(H1_W), blk(H2_W), blk(H2T_W)],
        out_shape=[jax.ShapeDtypeStruct((depth, H1_W, k), BF16),
                   jax.ShapeDtypeStruct((depth, H2_W, k), BF16),
                   jax.ShapeDtypeStruct((depth, H2T_W, k), BF16)],
        compiler_params=_cparams("parallel", "parallel"),
        name="w_in_prep",
    )(w_in_t)


def _mm_kernel(x_ref, wt_ref, o_ref):
    o_ref[...] = lax.dot_general(x_ref[...], wt_ref[...], _NT, preferred_element_type=F32).astype(o_ref.dtype)


def _matmul(x, wt, layer, out_dtype, tm, tn, name):
    m, k = x.shape
    n = wt.shape[1]
    tm = min(tm, m)
    return pl.pallas_call(
        _mm_kernel,
        grid=(m // tm, n // tn),
        in_specs=[pl.BlockSpec((tm, k), lambda i, j: (i, 0)),
                  pl.BlockSpec((None, tn, k), lambda i, j: (layer, j, 0))],
        out_specs=pl.BlockSpec((tm, tn), lambda i, j: (i, j)),
        out_shape=jax.ShapeDtypeStruct((m, n), out_dtype),
        compiler_params=_cparams("parallel", "parallel"),
        name=name,
    )(x, wt)


def _mm_nt_kernel(wt_ref, x_ref, scale_ref, o_ref):
    acc = lax.dot_general(wt_ref[...], x_ref[...], _NT, preferred_element_type=F32)
    o_ref[...] = (acc * scale_ref[...]).astype(o_ref.dtype)


def _matmul_nt(wt, layer, x, row_scale, out_dtype, tm, tn, name):
    n, k = wt.shape[1:]
    m = x.shape[0]
    tm = min(tm, m)
    return pl.pallas_call(
        _mm_nt_kernel,
        grid=(m // tm, n // tn),
        in_specs=[pl.BlockSpec((None, tn, k), lambda i, j: (layer, j, 0)),
                  pl.BlockSpec((tm, k), lambda i, j: (i, 0)),
                  pl.BlockSpec((tn, 1), lambda i, j: (j, 0))],
        out_specs=pl.BlockSpec((tn, tm), lambda i, j: (j, i)),
        out_shape=jax.ShapeDtypeStruct((n, m), out_dtype),
        compiler_params=_cparams("parallel", "parallel"),
        name=name,
    )(wt, x, row_scale)


def _rms(x, g):
    ms = jnp.mean(x * x, axis=-1, keepdims=True)
    return x * lax.rsqrt(ms + RMS_EPS) * g


def _mla_prep_kernel(lat_ref, qg_ref, kvg_ref, wqt_ref, wk_ref, wvt_ref, cos_ref, sin_ref, cos_t_ref, sin_t_ref,
                     qt_ref, k_ref, vt_ref):
    half = MLA_ROPE // 2
    cqn = _rms(lat_ref[:, 0:MLA_Q_RANK], qg_ref[...]).astype(BF16)
    ckvn = _rms(lat_ref[:, MLA_Q_RANK:MLA_Q_RANK + MLA_KV_RANK], kvg_ref[...]).astype(BF16)
    qa_t = lax.dot_general(wqt_ref[...], cqn, _NT, preferred_element_type=F32) * (MLA_SCALE * LOG2E)
    kn = jnp.dot(ckvn, wk_ref[...], preferred_element_type=F32)
    v_t = lax.dot_general(wvt_ref[...], ckvn, _NT, preferred_element_type=F32).astype(BF16)
    ones = jnp.ones((BF16_ROWS, v_t.shape[1]), BF16)
    cos_t = cos_t_ref[...]
    sin_t = sin_t_ref[...]
    kpe = lat_ref[:, MLA_Q_RANK + MLA_KV_RANK:MLA_Q_RANK + MLA_KV_RANK + LANE]
    lane = lax.broadcasted_iota(jnp.int32, kpe.shape, 1)
    kpe_rot = jnp.where(lane < half, -pltpu.roll(kpe, LANE - half, 1),
                        jnp.where(lane < MLA_ROPE, pltpu.roll(kpe, half, 1), 0.0))
    krot = (kpe * cos_ref[...] + kpe_rot * sin_ref[...]).astype(BF16)
    for h in range(MLA_HEADS):
        c0 = h * MLA_QK_PAD
        qt_ref[c0:c0 + LANE, :] = qa_t[c0:c0 + LANE, :].astype(BF16)
        qr = qa_t[c0 + LANE:c0 + 2 * LANE, :]
        qr_rot = jnp.concatenate([-qr[half:MLA_ROPE], qr[0:half], qr[MLA_ROPE:LANE]], axis=0)
        qt_ref[c0 + LANE:c0 + 2 * LANE, :] = (qr * cos_t + qr_rot * sin_t).astype(BF16)
        k_ref[:, c0:c0 + LANE] = kn[:, h * LANE:(h + 1) * LANE].astype(BF16)
        k_ref[:, c0 + LANE:c0 + 2 * LANE] = krot
        vt_ref[h, 0:MLA_V, :] = v_t[h * MLA_V:(h + 1) * MLA_V, :]
        vt_ref[h, MLA_V:MLA_V_AUG, :] = ones


def _mla_prep(h1, layer, qg, kvg, wqt, wk, wvt, cos, sin, cos_t, sin_t):
    s = h1.shape[0]
    tm = min(s, 512)
    nq = MLA_HEADS * MLA_QK_PAD
    full = lambda a: pl.BlockSpec((None,) + a.shape[1:], lambda i: (layer, 0, 0))
    rows = lambda w: pl.BlockSpec((tm, w), lambda i: (i, 0))
    cols = lambda w: pl.BlockSpec((w, tm), lambda i: (0, i))
    return pl.pallas_call(
        _mla_prep_kernel,
        grid=(s // tm,),
        in_specs=[rows(H1_LAT_W), full(qg), full(kvg), full(wqt), full(wk), full(wvt),
                  rows(LANE), rows(LANE), cols(LANE), cols(LANE)],
        out_specs=[cols(nq), rows(nq), pl.BlockSpec((MLA_HEADS, MLA_V_AUG, tm), lambda i: (0, 0, i))],
        out_shape=[jax.ShapeDtypeStruct((nq, s), BF16),
                   jax.ShapeDtypeStruct((s, nq), BF16),
                   jax.ShapeDtypeStruct((MLA_HEADS, MLA_V_AUG, s), BF16)],
        compiler_params=_cparams("parallel"),
        name="mla_prep",
    )(h1, qg, kvg, wqt, wk, wvt, cos, sin, cos_t, sin_t)


def _run_pipeline(n_items, stage_a, stage_b, stage_c_weights, stage_c_values):
    if n_items == 0:
        return

    def stage_c(t, slot):
        stage_c_weights(t, slot)
        stage_c_values(t, slot)

    def trip(i, slot):
        stage_c_weights(i - 2, slot)
        stage_a(i, slot)
        stage_c_values(i - 2, slot)
        stage_b(i - 1, 1 - slot)

    stage_a(0, 0)
    if n_items >= 2:
        stage_a(1, 1)
    stage_b(0, 0)
    steady = max(n_items - 2, 0)

    def body(ii, carry):
        i = 2 + 2 * ii
        trip(i, 0)
        trip(i + 1, 1)
        return carry

    lax.fori_loop(0, steady // 2, body, 0)
    if steady % 2:
        trip(n_items - 1, (n_items - 1) % 2)
    if n_items >= 2:
        stage_c_weights(n_items - 2, n_items % 2)
        stage_b(n_items - 1, (n_items - 1) % 2)
        stage_c_values(n_items - 2, n_items % 2)
    stage_c(n_items - 1, (n_items - 1) % 2)


def _mla_attn_kernel(qt_ref, k_ref, vt_ref, g_ref, o_ref, m_sc, acc_sc, s_buf, p_buf, al_buf, *, tq, tc):
    n = pl.program_id(1) + 1
    m_sc[...] = jnp.full(m_sc.shape, NEG_BIG, F32)
    acc_sc[...] = jnp.zeros(acc_sc.shape, F32)
    col_tiles = [slice(c * tc, (c + 1) * tc) for c in range(tq // tc)]

    def key_rows(u, n_keys):
        j = jnp.where(u == 0, n - 1, u - 1)
        return pl.ds(pl.multiple_of(j * tq, tq), n_keys)

    def stage_scores(u, slot):
        k = k_ref[key_rows(u, tq), :]
        for cols in col_tiles:
            s_buf[slot, :, cols] = jnp.dot(k, qt_ref[:, cols], preferred_element_type=F32)

    def stage_scores_diagonal(slot):
        for c, cols in enumerate(col_tiles):
            n_keys = (c + 1) * tc
            k = k_ref[key_rows(0, n_keys), :]
            s_buf[slot, 0:n_keys, cols] = jnp.dot(k, qt_ref[:, cols], preferred_element_type=F32)

    def stage_softmax(slot, masked):
        for c, cols in enumerate(col_tiles):
            n_keys = (c + 1) * tc if masked else tq
            s = s_buf[slot, 0:n_keys, cols]
            if masked:
                kc = lax.broadcasted_iota(jnp.int32, s.shape, 0) // CHUNK
                qc = (lax.broadcasted_iota(jnp.int32, s.shape, 1) + c * tc) // CHUNK
                s = jnp.where(kc <= qc, s, NEG_BIG)
            m_prev = m_sc[:, cols]
            m_new = jnp.maximum(m_prev, jnp.max(s, axis=0, keepdims=True))
            p_buf[slot, 0:n_keys, cols] = jnp.exp2(s - m_new).astype(BF16)
            if n_keys < tq:
                p_buf[slot, n_keys:tq, cols] = jnp.zeros((tq - n_keys, tc), BF16)
            al_buf[slot, :, cols] = jnp.exp2(m_prev - m_new)
            m_sc[:, cols] = m_new

    def stage_values(u, slot):
        vt = vt_ref[:, key_rows(u, tq)]
        for cols in col_tiles:
            pv = jnp.dot(vt, p_buf[slot, :, cols], preferred_element_type=F32)
            acc_sc[:, cols] = al_buf[slot, :, cols] * acc_sc[:, cols] + pv

    def trip(i, slot):
        stage_scores(i, slot)
        stage_values(i - 2, slot)
        stage_softmax(1 - slot, False)

    def drain(last_slot):
        stage_values(n - 2, 1 - last_slot)
        stage_softmax(last_slot, False)
        stage_values(n - 1, last_slot)

    stage_scores_diagonal(0)

    @pl.when(n >= 2)
    def _():
        stage_softmax(0, True)
        stage_scores(1, 1)

    def body(ii, carry):
        i = 2 + 2 * ii
        trip(i, 0)
        trip(i + 1, 1)
        return carry

    lax.fori_loop(0, (n - 2) // 2, body, 0)
    n_odd = n % 2 == 1

    @pl.when(jnp.logical_and(n_odd, n >= 3))
    def _():
        trip(n - 1, 0)
        drain(0)

    @pl.when(jnp.logical_not(n_odd))
    def _():
        drain(1)

    @pl.when(n == 1)
    def _():
        stage_softmax(0, True)
        stage_values(0, 0)

    o = (acc_sc[0:MLA_V, :] / acc_sc[MLA_V:MLA_V + 1, :]).T
    o_ref[...] = (o * jax.nn.silu(g_ref[...])).astype(o_ref.dtype)


def _pair_table(nq):
    pairs = [(qi, qi - d) for d in range(1, nq) for qi in range(d, nq)] or [(0, 0)]
    return jnp.asarray(pairs, jnp.int32).T


def _mla_attn(qt, k, vt, h1):
    s = k.shape[0]
    tq = min(MLA_TQ, s)
    tc = min(ATT_TC, tq)
    ga_blk = H1_GA // LANE
    return pl.pallas_call(
        functools.partial(_mla_attn_kernel, tq=tq, tc=tc),
        grid=(MLA_HEADS, s // tq),
        in_specs=[pl.BlockSpec((MLA_QK_PAD, tq), lambda h, i: (h, i)),
                  pl.BlockSpec((s, MLA_QK_PAD), lambda h, i: (0, h)),
                  pl.BlockSpec((None, MLA_V_AUG, s), lambda h, i: (h, 0, 0)),
                  pl.BlockSpec((tq, LANE), lambda h, i: (i, ga_blk + h))],
        out_specs=pl.BlockSpec((tq, MLA_V), lambda h, i: (i, h)),
        out_shape=jax.ShapeDtypeStruct((s, MLA_WIDTH), BF16),
        scratch_shapes=[pltpu.VMEM((1, tq), F32), pltpu.VMEM((MLA_V_AUG, tq), F32),
                        pltpu.VMEM((2, tq, tq), F32), pltpu.VMEM((2, tq, tq), BF16), pltpu.VMEM((2, 1, tq), F32)],
        compiler_params=_cparams("parallel", "parallel", flags=ATT_FLAGS),
        name="mla_attn",
    )(qt, k, vt, h1)


def _sb_attn_kernel(tab_ref, qt_ref, k_ref, vt_ref, tri_ref, g_ref, o_ref, r_st, acc_st,
                    lb_buf, hi_buf, x_buf, cs_buf, a_buf, *, tq, tc, nq):
    n_sub = tq // tc
    n_off = nq * (nq - 1) // 2
    r_st[...] = jnp.zeros(r_st.shape, F32)
    acc_st[...] = jnp.zeros(acc_st.shape, F32)
    sub = [slice(t * tc, (t + 1) * tc) for t in range(n_sub)]
    strict = (lax.broadcasted_iota(jnp.int32, (tc, tc), 0)
              < lax.broadcasted_iota(jnp.int32, (tc, tc), 1))

    def q_cols(qi, c):
        return pl.ds(pl.multiple_of(qi * tq + c * tc, tc), tc)

    def stage_a(qi, j, slot, diag):
        for kk in range(n_sub):
            k = k_ref[pl.ds(pl.multiple_of(j * tq + kk * tc, tc), tc), :]
            for c in range(n_sub):
                if diag and kk > c:
                    hi_buf[slot, sub[kk], sub[c]] = jnp.zeros((tc, tc), BF16)
                    continue
                y = jnp.dot(k, qt_ref[:, q_cols(qi, c)], preferred_element_type=F32)
                t = jnp.log(1.0 + jnp.exp2(-jnp.abs(y))) * LOG2E
                neg_part = jnp.minimum(y, 0.0)
                log_1mb = (neg_part - y) - t
                if diag and kk == c:
                    log_1mb = jnp.where(strict, log_1mb, 0.0)
                lb_buf[slot, sub[kk], sub[c]] = neg_part - t
                hi_buf[slot, sub[kk], sub[c]] = log_1mb.astype(BF16)

    def stage_b(slot):
        tri = tri_ref[...]
        for kk in range(n_sub):
            for c in range(n_sub):
                r = jnp.dot(tri, hi_buf[slot, sub[kk], sub[c]], preferred_element_type=F32)
                x_buf[slot, sub[kk], sub[c]] = r[:tc] + lb_buf[slot, sub[kk], sub[c]]
                cs_buf[slot, kk, :, sub[c]] = r[tc:tc + 1]

    def stage_c_weights(qi, slot, diag):
        for c in range(n_sub):
            run = r_st[:, q_cols(qi, c)]
            for kk in range(n_sub - 1, -1, -1):
                if diag and kk > c:
                    a_buf[sub[kk], sub[c]] = jnp.zeros((tc, tc), BF16)
                    continue
                a = jnp.exp2(x_buf[slot, sub[kk], sub[c]] + run)
                if diag and kk == c:
                    a = jnp.where(strict, a, 0.0)
                a_buf[sub[kk], sub[c]] = a.astype(BF16)
                run = run + cs_buf[slot, kk, :, sub[c]]
            r_st[:, q_cols(qi, c)] = run

    def stage_c_values(qi, j):
        vt = vt_ref[:, pl.ds(pl.multiple_of(j * tq, tq), tq)]
        for c in range(n_sub):
            acc_st[:, q_cols(qi, c)] += jnp.dot(vt, a_buf[:, sub[c]], preferred_element_type=F32)

    _run_pipeline(nq,
                  lambda t, slot: stage_a(t, t, slot, True),
                  lambda t, slot: stage_b(slot),
                  lambda t, slot: stage_c_weights(t, slot, True),
                  lambda t, slot: stage_c_values(t, t))
    _run_pipeline(n_off,
                  lambda t, slot: stage_a(tab_ref[0, t], tab_ref[1, t], slot, False),
                  lambda t, slot: stage_b(slot),
                  lambda t, slot: stage_c_weights(tab_ref[0, t], slot, False),
                  lambda t, slot: stage_c_values(tab_ref[0, t], tab_ref[1, t]))

    def finish(qi, carry):
        rows = pl.ds(pl.multiple_of(qi * tq, tq), tq)
        o_ref[rows, :] = (acc_st[:, rows].T * jax.nn.silu(g_ref[rows, :])).astype(o_ref.dtype)
        return carry

    lax.fori_loop(0, nq, finish, 0)


def _sb_attn(h2, h2t, h1, tri):
    s = h2.shape[0]
    tq = min(SB_TQ, s)
    tc = tri.shape[1]
    nq = s // tq
    qb, kb, vb, gb = H2T_SBQ // LANE, H2_SBK // LANE, H2T_SBV // LANE, H1_GC // LANE
    return pl.pallas_call(
        functools.partial(_sb_attn_kernel, tq=tq, tc=tc, nq=nq),
        grid=(SB_HEADS,),
        in_specs=[pl.BlockSpec(memory_space=pltpu.SMEM),
                  pl.BlockSpec((LANE, s), lambda h: (qb + h, 0)),
                  pl.BlockSpec((s, LANE), lambda h: (0, kb + h)),
                  pl.BlockSpec((LANE, s), lambda h: (vb + h, 0)),
                  pl.BlockSpec(tri.shape, lambda h: (0, 0)),
                  pl.BlockSpec((s, LANE), lambda h: (0, gb + h))],
        out_specs=pl.BlockSpec((s, LANE), lambda h: (0, h)),
        out_shape=jax.ShapeDtypeStruct((s, SB_WIDTH), BF16),
        scratch_shapes=[pltpu.VMEM((1, s), F32), pltpu.VMEM((SB_HEAD_DIM, s), F32),
                        pltpu.VMEM((2, tq, tq), F32), pltpu.VMEM((2, tq, tq), BF16),
                        pltpu.VMEM((2, tq, tq), F32), pltpu.VMEM((2, tq // tc, 1, tq), F32),
                        pltpu.VMEM((tq, tq), BF16)],
        compiler_params=_cparams("parallel", flags=ATT_FLAGS),
        name="sb_attn",
    )(_pair_table(nq), h2t, h2, h2t, tri, h1)


def _gmlp_kernel(u_ref, v_ref, g_ref, lng_ref, lnb_ref, w_ref, b_ref, o_ref, *, n_chunks):
    gv = jax.nn.gelu(v_ref[...])
    mu = jnp.mean(gv, axis=-1, keepdims=True)
    vc = gv - mu
    var = jnp.mean(vc * vc, axis=-1, keepdims=True)
    vn = (vc * lax.rsqrt(var + LN_EPS) * lng_ref[...] + lnb_ref[...]).astype(BF16)
    t_chunk = lax.broadcasted_iota(jnp.int32, (SG_CHUNK, SG_CHUNK), 0) // CHUNK
    s_chunk = lax.broadcasted_iota(jnp.int32, (SG_CHUNK, SG_CHUNK), 1) // CHUNK
    mask = (s_chunk <= t_chunk).astype(F32)
    for g in range(SG_GROUPS):
        w_sp = (w_ref[g] * mask).astype(BF16)
        bias = b_ref[:, g:g + 1]
        cols = slice(g * SG_GROUP_CH, (g + 1) * SG_GROUP_CH)
        for n in range(n_chunks):
            rows = slice(n * SG_CHUNK, (n + 1) * SG_CHUNK)
            mixed = jnp.dot(w_sp, vn[rows, cols], preferred_element_type=F32) + bias
            o_b = jax.nn.gelu(u_ref[rows, cols]) * mixed
            o_ref[rows, cols] = (o_b * jax.nn.silu(g_ref[rows, cols])).astype(o_ref.dtype)


def _gmlp(h1, layer, lng, lnb, w, b_t):
    s = h1.shape[0]
    tm = min(s, 512)
    full = lambda a: pl.BlockSpec((None,) + a.shape[1:], lambda i: (layer,) + (0,) * (a.ndim - 1))
    return pl.pallas_call(
        functools.partial(_gmlp_kernel, n_chunks=tm // SG_CHUNK),
        grid=(s // tm,),
        in_specs=[pl.BlockSpec((tm, SG_WIDTH), lambda i: (i, H1_SGU // SG_WIDTH)),
                  pl.BlockSpec((tm, SG_WIDTH), lambda i: (i, H1_SGV // SG_WIDTH)),
                  pl.BlockSpec((tm, SG_WIDTH), lambda i: (i, H1_GB // SG_WIDTH)),
                  full(lng), full(lnb), full(w), full(b_t)],
        out_specs=pl.BlockSpec((tm, SG_WIDTH), lambda i: (i, 0)),
        out_shape=jax.ShapeDtypeStruct((s, SG_WIDTH), BF16),
        compiler_params=_cparams("parallel"),
        name="gmlp",
    )(h1, h1, h1, lng, lnb, w, b_t)


def _mem_kv_kernel(mem_ref, wk_ref, wv_ref, kbd_ref, vbd_ref):
    mem = mem_ref[...].astype(BF16)
    mk_t = jnp.dot(mem, wk_ref[...], preferred_element_type=F32).T
    mv = jnp.dot(mem, wv_ref[...], preferred_element_type=F32)
    feat_row = lax.broadcasted_iota(jnp.int32, mk_t.shape, 0) // MEM_HEAD_DIM
    feat_col = lax.broadcasted_iota(jnp.int32, mv.shape, 1) // MEM_HEAD_DIM
    for h in range(MEM_HEADS):
        seg = slice(h * MEM_TOKENS, (h + 1) * MEM_TOKENS)
        kbd_ref[:, seg] = jnp.where(feat_row == h, mk_t, 0.0).astype(BF16)
        vbd_ref[seg, :] = jnp.where(feat_col == h, mv, 0.0).astype(BF16)


def _mem_kv(mem, layer, wk, wv):
    nt = MEM_HEADS * MEM_TOKENS
    full2 = lambda shape: pl.BlockSpec(shape, lambda i: (0, 0))
    stack = lambda a: pl.BlockSpec((None,) + a.shape[1:], lambda i: (layer, 0, 0))
    return pl.pallas_call(
        _mem_kv_kernel,
        grid=(1,),
        in_specs=[full2(mem.shape), stack(wk), stack(wv)],
        out_specs=[full2((MEM_WIDTH, nt)), full2((nt, MEM_WIDTH))],
        out_shape=[jax.ShapeDtypeStruct((MEM_WIDTH, nt), BF16),
                   jax.ShapeDtypeStruct((nt, MEM_WIDTH), BF16)],
        compiler_params=_cparams("arbitrary"),
        name="mem_kv",
    )(mem, wk, wv)


def _mem_attn_kernel(q_ref, kbd_ref, vbd_ref, g_ref, o_ref):
    logits = jnp.dot(q_ref[...], kbd_ref[...], preferred_element_type=F32) * MEM_SCALE
    probs = []
    for h in range(MEM_HEADS):
        seg = logits[:, h * MEM_TOKENS:(h + 1) * MEM_TOKENS]
        e = jnp.exp(seg - jnp.max(seg, axis=-1, keepdims=True))
        probs.append((e / jnp.sum(e, axis=-1, keepdims=True)).astype(BF16))
    p = jnp.concatenate(probs, axis=-1)
    o = jnp.dot(p, vbd_ref[...], preferred_element_type=F32)
    o_ref[...] = (o * jax.nn.silu(g_ref[...])).astype(o_ref.dtype)


def _mem_attn(h2, kbd, vbd, h1):
    s = h2.shape[0]
    tm = min(s, 512)
    full2 = lambda shape: pl.BlockSpec(shape, lambda i: (0, 0))
    return pl.pallas_call(
        _mem_attn_kernel,
        grid=(s // tm,),
        in_specs=[pl.BlockSpec((tm, MEM_WIDTH), lambda i: (i, H2_MQ // MEM_WIDTH)),
                  full2(kbd.shape), full2(vbd.shape),
                  pl.BlockSpec((tm, MEM_WIDTH), lambda i: (i, H1_GM // MEM_WIDTH))],
        out_specs=pl.BlockSpec((tm, MEM_WIDTH), lambda i: (i, 0)),
        out_shape=jax.ShapeDtypeStruct((s, MEM_WIDTH), BF16),
        compiler_params=_cparams("parallel"),
        name="mem_attn",
    )(h2, kbd, vbd, h1)


def _out_ln_kernel(ya_ref, yb_ref, yc_ref, ym_ref, w_ref, x_ref, g_ref, b_ref, o_ref, obf_ref, *, alpha):
    tm = x_ref.shape[0]
    for rows in (slice(0, tm // 2), slice(tm // 2, tm)):
        y = None
        row = 0
        for y_ref in (ya_ref, yb_ref, yc_ref, ym_ref):
            width = y_ref.shape[1]
            part = jnp.dot(y_ref[rows, :], w_ref[row:row + width, :], preferred_element_type=F32)
            y = part if y is None else y + part
            row += width
        r = alpha * x_ref[rows, :] + y
        mu = jnp.mean(r, axis=-1, keepdims=True)
        rc = r - mu
        var = jnp.mean(rc * rc, axis=-1, keepdims=True)
        out = rc * lax.rsqrt(var + LN_EPS) * g_ref[...] + b_ref[...]
        o_ref[rows, :] = out
        obf_ref[rows, :] = out.astype(BF16)


def _out_ln(ya, yb, yc, ym, w, layer, x, g, b, alpha):
    s, d = x.shape
    tm = min(s, 512)
    row = lambda w: pl.BlockSpec((tm, w), lambda i: (i, 0))
    full = lambda a: pl.BlockSpec((None,) + a.shape[1:], lambda i: (layer, 0, 0))
    return pl.pallas_call(
        functools.partial(_out_ln_kernel, alpha=alpha),
        grid=(s // tm,),
        in_specs=[row(MLA_WIDTH), row(SG_WIDTH), row(SB_WIDTH), row(MEM_WIDTH),
                  full(w), row(d), full(g), full(b)],
        out_specs=[row(d), row(d)],
        out_shape=[jax.ShapeDtypeStruct((s, d), F32), jax.ShapeDtypeStruct((s, d), BF16)],
        compiler_params=_cparams("parallel"),
        name="out_ln",
    )(ya, yb, yc, ym, w, x, g, b)


def _prep_mla_weights(w_uq, w_ukv):
    depth = w_uq.shape[0]
    wq = w_uq.reshape(depth, MLA_Q_RANK, MLA_HEADS, MLA_NOPE + MLA_ROPE)
    wq = jnp.pad(wq, ((0, 0), (0, 0), (0, 0), (0, MLA_QK_PAD - MLA_NOPE - MLA_ROPE)))
    wqt = jnp.swapaxes(wq.reshape(depth, MLA_Q_RANK, MLA_HEADS * MLA_QK_PAD), 1, 2).astype(BF16)
    wkv = w_ukv.reshape(depth, MLA_KV_RANK, MLA_HEADS, MLA_NOPE + MLA_V)
    wk = wkv[..., :MLA_NOPE].reshape(depth, MLA_KV_RANK, MLA_WIDTH).astype(BF16)
    wvt = jnp.swapaxes(wkv[..., MLA_NOPE:].reshape(depth, MLA_KV_RANK, MLA_WIDTH), 1, 2).astype(BF16)
    return wqt, wk, wvt


def kernel(x, mem, positions, w_in, q_norm_g, w_uq, kv_norm_g, w_ukv, sg_ln_g, sg_ln_b, sg_w, sg_b,
           w_mem_k, w_mem_v, w_out, ln_g, ln_b):
    b, s, d = x.shape
    depth = w_in.shape[0]
    alpha = (2.0 * depth) ** 0.25
    sb_tc = min(s, ATT_TC)

    inv_freq = ROPE_THETA ** (-jnp.arange(0, MLA_ROPE, 2, dtype=F32) / MLA_ROPE)
    invf = jnp.concatenate([inv_freq, inv_freq, jnp.zeros((LANE - MLA_ROPE,), F32)])
    tri = jnp.concatenate(
        [(lax.broadcasted_iota(jnp.int32, (sb_tc, sb_tc), 1)
          > lax.broadcasted_iota(jnp.int32, (sb_tc, sb_tc), 0)).astype(BF16),
         jnp.ones((BF16_ROWS, sb_tc), BF16)], axis=0)
    w1, w2, w2q = _w_in_prep(jnp.swapaxes(w_in, 1, 2))
    wqt, wk, wvt = _prep_mla_weights(w_uq, w_ukv)
    wmk, wmv, wo = w_mem_k.astype(BF16), w_mem_v.astype(BF16), w_out.astype(BF16)
    row3 = lambda p: p[:, None, :]
    sg_b_t = jnp.swapaxes(sg_b, 1, 2)
    h2t_scale = jnp.concatenate([jnp.full((SB_WIDTH, 1), SB_SCALE * LOG2E, F32),
                                 jnp.ones((SB_WIDTH, 1), F32)], axis=0)

    outs = []
    for bi in range(b):
        xf = x[bi]
        xb = xf.astype(BF16)
        cos, sin, cos_t, sin_t = _rope_tables(positions[bi].reshape(s, 1), positions[bi].reshape(1, s),
                                              invf[None, :], invf[:, None])
        for l in range(depth):
            h1 = _matmul(xb, w1, l, F32, 1024, 1024, "in_proj_f32")
            h2 = _matmul(xb, w2, l, BF16, 1024, H2_W, "in_proj_bf16")
            h2t = _matmul_nt(w2q, l, xb, h2t_scale, BF16, 1024, H2T_W, "in_proj_bf16_t")
            qt, k, vt = _mla_prep(h1, l, row3(q_norm_g), row3(kv_norm_g), wqt, wk, wvt, cos, sin, cos_t, sin_t)
            ya = _mla_attn(qt, k, vt, h1)
            yc = _sb_attn(h2, h2t, h1, tri)
            yb = _gmlp(h1, l, row3(sg_ln_g), row3(sg_ln_b), sg_w, sg_b_t)
            kbd, vbd = _mem_kv(mem[bi], l, wmk, wmv)
            ym = _mem_attn(h2, kbd, vbd, h1)
            xf, xb = _out_ln(ya, yb, yc, ym, wo, l, xf, row3(ln_g), row3(ln_b), alpha)
        outs.append(xf)
    return outs[0][None] if b == 1 else jnp.stack(outs, axis=0)
```

```python
import functools
import math

import jax
import jax.numpy as jnp
from jax import lax
from jax.experimental import pallas as pl
from jax.experimental.pallas import tpu as pltpu

F32 = jnp.float32
BF16 = jnp.bfloat16

D_MODEL = 2048
CHUNK = 64
MLA_HEADS = 6
MLA_NOPE = 128
MLA_ROPE = 64
MLA_V = 128
MLA_Q_RANK = 512
MLA_KV_RANK = 256
MLA_WIDTH = MLA_HEADS * MLA_V
ROPE_THETA = 10000.0
SG_GROUPS = 4
SG_GROUP_CH = 128
SG_WIDTH = SG_GROUPS * SG_GROUP_CH
SG_CHUNK = 128
SB_HEADS = 4
SB_HEAD_DIM = 128
SB_WIDTH = SB_HEADS * SB_HEAD_DIM
MEM_TOKENS = 256
MEM_HEADS = 4
MEM_HEAD_DIM = 64
MEM_WIDTH = MEM_HEADS * MEM_HEAD_DIM
LN_EPS = 1e-5
RMS_EPS = 1e-6

LOG2E = math.log2(math.e)
MLA_SCALE = 1.0 / math.sqrt(MLA_NOPE + MLA_ROPE)
SB_SCALE = 1.0 / math.sqrt(SB_HEAD_DIM)
MEM_SCALE = 1.0 / math.sqrt(MEM_HEAD_DIM)

LANE = 128
BF16_ROWS = 16
MXU_TILE = 256
MLA_QK_PAD = 256
MLA_V_AUG = MLA_V + BF16_ROWS
NEG_BIG = -1e30
VMEM_LIMIT = 48 * 1024 * 1024
MLA_TQ = 1024
SB_TQ = 512
ATT_TC = MXU_TILE

H1_LAT_W = 1024
H1_SGU = 1024
H1_SGV = 1536
H1_GB = 2048
H1_GA = 2560
H1_GC = 3328
H1_GM = 3840
H1_W = 4096
H2_SBK = 0
H2_MQ = 512
H2_W = 768
H2T_SBQ = 0
H2T_SBV = 512
H2T_W = 1024

_NT = (((1,), (1,)), ((), ()))


def _cparams(*sem, flags=None):
    return pltpu.CompilerParams(dimension_semantics=sem, vmem_limit_bytes=VMEM_LIMIT, flags=flags)


ATT_FLAGS = None


def _rope_table_kernel(pos_col_ref, pos_row_ref, invf_row_ref, invf_col_ref, cos_ref, sin_ref, cos_t_ref, sin_t_ref):
    ang = pos_col_ref[...].astype(F32) * invf_row_ref[...]
    cos_ref[...] = jnp.cos(ang)
    sin_ref[...] = jnp.sin(ang)
    ang_t = invf_col_ref[...] * pos_row_ref[...].astype(F32)
    cos_t_ref[...] = jnp.cos(ang_t)
    sin_t_ref[...] = jnp.sin(ang_t)


def _rope_tables(pos_col, pos_row, invf_row, invf_col):
    s = pos_col.shape[0]
    tm = min(s, 1024)
    return pl.pallas_call(
        _rope_table_kernel,
        grid=(s // tm,),
        in_specs=[pl.BlockSpec((tm, 1), lambda i: (i, 0)),
                  pl.BlockSpec((1, tm), lambda i: (0, i)),
                  pl.BlockSpec((1, LANE), lambda i: (0, 0)),
                  pl.BlockSpec((LANE, 1), lambda i: (0, 0))],
        out_specs=[pl.BlockSpec((tm, LANE), lambda i: (i, 0)),
                   pl.BlockSpec((tm, LANE), lambda i: (i, 0)),
                   pl.BlockSpec((LANE, tm), lambda i: (0, i)),
                   pl.BlockSpec((LANE, tm), lambda i: (0, i))],
        out_shape=[jax.ShapeDtypeStruct((s, LANE), F32)] * 2 + [jax.ShapeDtypeStruct((LANE, s), F32)] * 2,
        compiler_params=_cparams("parallel"),
        name="rope_tables",
    )(pos_col, pos_row, invf_row, invf_col)


_IN_SEGMENTS = ("c_q", "c_kv", "k_pe", "g_a", "sg_u", "sg_v", "g_b", "sb_q", "sb_k", "sb_v", "g_c", "m_q", "g_m")
_IN_WIDTHS = (MLA_Q_RANK, MLA_KV_RANK, MLA_ROPE, MLA_WIDTH, SG_WIDTH, SG_WIDTH, SG_WIDTH,
              SB_WIDTH, SB_WIDTH, SB_WIDTH, SB_WIDTH, MEM_WIDTH, MEM_WIDTH)
_IN_OFFSETS = tuple(sum(_IN_WIDTHS[:i]) for i in range(len(_IN_WIDTHS) + 1))
IN_ROWS = {name: (_IN_OFFSETS[i], _IN_OFFSETS[i + 1]) for i, name in enumerate(_IN_SEGMENTS)}


def _w_in_prep_kernel(wt_ref, w1_ref, w2_ref, w2q_ref):
    def put(dst, off, name):
        a, b = IN_ROWS[name]
        dst[off:off + b - a, :] = wt_ref[a:b, :].astype(BF16)

    lat_end = MLA_Q_RANK + MLA_KV_RANK + MLA_ROPE
    for name, off in (("c_q", 0), ("c_kv", MLA_Q_RANK), ("k_pe", MLA_Q_RANK + MLA_KV_RANK),
                      ("sg_u", H1_SGU), ("sg_v", H1_SGV), ("g_b", H1_GB), ("g_a", H1_GA), ("g_c", H1_GC),
                      ("g_m", H1_GM)):
        put(w1_ref, off, name)
    w1_ref[lat_end:H1_LAT_W, :] = jnp.zeros((H1_LAT_W - lat_end, w1_ref.shape[1]), BF16)
    put(w2_ref, H2_SBK, "sb_k")
    put(w2_ref, H2_MQ, "m_q")
    put(w2q_ref, H2T_SBQ, "sb_q")
    put(w2q_ref, H2T_SBV, "sb_v")


def _w_in_prep(w_in_t):
    depth, n, k = w_in_t.shape
    tk = 256
    blk = lambda rows: pl.BlockSpec((None, rows, tk), lambda l, i: (l, 0, i))
    return pl.pallas_call(
        _w_in_prep_kernel,
        grid=(depth, k // tk),
        in_specs=[blk(n)],
        out_specs=[blk(H1_W), blk(H2_W), blk(H2T_W)],
        out_shape=[jax.ShapeDtypeStruct((depth, H1_W, k), BF16),
                   jax.ShapeDtypeStruct((depth, H2_W, k), BF16),
                   jax.ShapeDtypeStruct((depth, H2T_W, k), BF16)],
        compiler_params=_cparams("parallel", "parallel"),
        name="w_in_prep",
    )(w_in_t)


def _mm_kernel(x_ref, wt_ref, o_ref):
    o_ref[...] = lax.dot_general(x_ref[...], wt_ref[...], _NT, preferred_element_type=F32).astype(o_ref.dtype)


def _matmul(x, wt, layer, out_dtype, tm, tn, name):
    m, k = x.shape
    n = wt.shape[1]
    tm = min(tm, m)
    return pl.pallas_call(
        _mm_kernel,
        grid=(m // tm, n // tn),
        in_specs=[pl.BlockSpec((tm, k), lambda i, j: (i, 0)),
                  pl.BlockSpec((None, tn, k), lambda i, j: (layer, j, 0))],
        out_specs=pl.BlockSpec((tm, tn), lambda i, j: (i, j)),
        out_shape=jax.ShapeDtypeStruct((m, n), out_dtype),
        compiler_params=_cparams("parallel", "parallel"),
        name=name,
    )(x, wt)


def _mm_nt_kernel(wt_ref, x_ref, scale_ref, o_ref):
    acc = lax.dot_general(wt_ref[...], x_ref[...], _NT, preferred_element_type=F32)
    o_ref[...] = (acc * scale_ref[...]).astype(o_ref.dtype)


def _matmul_nt(wt, layer, x, row_scale, out_dtype, tm, tn, name):
    n, k = wt.shape[1:]
    m = x.shape[0]
    tm = min(tm, m)
    return pl.pallas_call(
        _mm_nt_kernel,
        grid=(m // tm, n // tn),
        in_specs=[pl.BlockSpec((None, tn, k), lambda i, j: (layer, j, 0)),
                  pl.BlockSpec((tm, k), lambda i, j: (i, 0)),
                  pl.BlockSpec((tn, 1), lambda i, j: (j, 0))],
        out_specs=pl.BlockSpec((tn, tm), lambda i, j: (j, i)),
        out_shape=jax.ShapeDtypeStruct((n, m), out_dtype),
        compiler_params=_cparams("parallel", "parallel"),
        name=name,
    )(wt, x, row_scale)


def _rms(x, g):
    ms = jnp.mean(x * x, axis=-1, keepdims=True)
    return x * lax.rsqrt(ms + RMS_EPS) * g


def _mla_prep_kernel(lat_ref, qg_ref, kvg_ref, wqt_ref, wk_ref, wvt_ref, cos_ref, sin_ref, cos_t_ref, sin_t_ref,
                     qt_ref, k_ref, vt_ref):
    half = MLA_ROPE // 2
    cqn = _rms(lat_ref[:, 0:MLA_Q_RANK], qg_ref[...]).astype(BF16)
    ckvn = _rms(lat_ref[:, MLA_Q_RANK:MLA_Q_RANK + MLA_KV_RANK], kvg_ref[...]).astype(BF16)
    qa_t = lax.dot_general(wqt_ref[...], cqn, _NT, preferred_element_type=F32) * (MLA_SCALE * LOG2E)
    kn = jnp.dot(ckvn, wk_ref[...], preferred_element_type=F32)
    v_t = lax.dot_general(wvt_ref[...], ckvn, _NT, preferred_element_type=F32).astype(BF16)
    ones = jnp.ones((BF16_ROWS, v_t.shape[1]), BF16)
    cos_t = cos_t_ref[...]
    sin_t = sin_t_ref[...]
    kpe = lat_ref[:, MLA_Q_RANK + MLA_KV_RANK:MLA_Q_RANK + MLA_KV_RANK + LANE]
    lane = lax.broadcasted_iota(jnp.int32, kpe.shape, 1)
    kpe_rot = jnp.where(lane < half, -pltpu.roll(kpe, LANE - half, 1),
                        jnp.where(lane < MLA_ROPE, pltpu.roll(kpe, half, 1), 0.0))
    krot = (kpe * cos_ref[...] + kpe_rot * sin_ref[...]).astype(BF16)
    for h in range(MLA_HEADS):
        c0 = h * MLA_QK_PAD
        qt_ref[c0:c0 + LANE, :] = qa_t[c0:c0 + LANE, :].astype(BF16)
        qr = qa_t[c0 + LANE:c0 + 2 * LANE, :]
        qr_rot = jnp.concatenate([-qr[half:MLA_ROPE], qr[0:half], qr[MLA_ROPE:LANE]], axis=0)
        qt_ref[c0 + LANE:c0 + 2 * LANE, :] = (qr * cos_t + qr_rot * sin_t).astype(BF16)
        k_ref[:, c0:c0 + LANE] = kn[:, h * LANE:(h + 1) * LANE].astype(BF16)
        k_ref[:, c0 + LANE:c0 + 2 * LANE] = krot
        vt_ref[h, 0:MLA_V, :] = v_t[h * MLA_V:(h + 1) * MLA_V, :]
        vt_ref[h, MLA_V:MLA_V_AUG, :] = ones


def _mla_prep(h1, layer, qg, kvg, wqt, wk, wvt, cos, sin, cos_t, sin_t):
    s = h1.shape[0]
    tm = min(s, 512)
    nq = MLA_HEADS * MLA_QK_PAD
    full = lambda a: pl.BlockSpec((None,) + a.shape[1:], lambda i: (layer, 0, 0))
    rows = lambda w: pl.BlockSpec((tm, w), lambda i: (i, 0))
    cols = lambda w: pl.BlockSpec((w, tm), lambda i: (0, i))
    return pl.pallas_call(
        _mla_prep_kernel,
        grid=(s // tm,),
        in_specs=[rows(H1_LAT_W), full(qg), full(kvg), full(wqt), full(wk), full(wvt),
                  rows(LANE), rows(LANE), cols(LANE), cols(LANE)],
        out_specs=[cols(nq), rows(nq), pl.BlockSpec((MLA_HEADS, MLA_V_AUG, tm), lambda i: (0, 0, i))],
        out_shape=[jax.ShapeDtypeStruct((nq, s), BF16),
                   jax.ShapeDtypeStruct((s, nq), BF16),
                   jax.ShapeDtypeStruct((MLA_HEADS, MLA_V_AUG, s), BF16)],
        compiler_params=_cparams("parallel"),
        name="mla_prep",
    )(h1, qg, kvg, wqt, wk, wvt, cos, sin, cos_t, sin_t)


def _run_pipeline(n_items, stage_a, stage_b, stage_c_weights, stage_c_values):
    if n_items == 0:
        return

    def stage_c(t, slot):
        stage_c_weights(t, slot)
        stage_c_values(t, slot)

    def trip(i, slot):
        stage_c_weights(i - 2, slot)
        stage_a(i, slot)
        stage_c_values(i - 2, slot)
        stage_b(i - 1, 1 - slot)

    stage_a(0, 0)
    if n_items >= 2:
        stage_a(1, 1)
    stage_b(0, 0)
    steady = max(n_items - 2, 0)

    def body(ii, carry):
        i = 2 + 2 * ii
        trip(i, 0)
        trip(i + 1, 1)
        return carry

    lax.fori_loop(0, steady // 2, body, 0)
    if steady % 2:
        trip(n_items - 1, (n_items - 1) % 2)
    if n_items >= 2:
        stage_c_weights(n_items - 2, n_items % 2)
        stage_b(n_items - 1, (n_items - 1) % 2)
        stage_c_values(n_items - 2, n_items % 2)
    stage_c(n_items - 1, (n_items - 1) % 2)


def _mla_attn_kernel(qt_ref, k_ref, vt_ref, g_ref, o_ref, m_sc, acc_sc, s_buf, p_buf, al_buf, *, tq, tc):
    n = pl.program_id(1) + 1
    m_sc[...] = jnp.full(m_sc.shape, NEG_BIG, F32)
    acc_sc[...] = jnp.zeros(acc_sc.shape, F32)
    col_tiles = [slice(c * tc, (c + 1) * tc) for c in range(tq // tc)]

    def key_rows(u, n_keys):
        j = jnp.where(u == 0, n - 1, u - 1)
        return pl.ds(pl.multiple_of(j * tq, tq), n_keys)

    def stage_scores(u, slot):
        k = k_ref[key_rows(u, tq), :]
        for cols in col_tiles:
            s_buf[slot, :, cols] = jnp.dot(k, qt_ref[:, cols], preferred_element_type=F32)

    def stage_scores_diagonal(slot):
        for c, cols in enumerate(col_tiles):
            n_keys = (c + 1) * tc
            k = k_ref[key_rows(0, n_keys), :]
            s_buf[slot, 0:n_keys, cols] = jnp.dot(k, qt_ref[:, cols], preferred_element_type=F32)

    def stage_softmax(slot, masked):
        for c, cols in enumerate(col_tiles):
            n_keys = (c + 1) * tc if masked else tq
            s = s_buf[slot, 0:n_keys, cols]
            if masked:
                kc = lax.broadcasted_iota(jnp.int32, s.shape, 0) // CHUNK
                qc = (lax.broadcasted_iota(jnp.int32, s.shape, 1) + c * tc) // CHUNK
                s = jnp.where(kc <= qc, s, NEG_BIG)
            m_prev = m_sc[:, cols]
            m_new = jnp.maximum(m_prev, jnp.max(s, axis=0, keepdims=True))
            p_buf[slot, 0:n_keys, cols] = jnp.exp2(s - m_new).astype(BF16)
            if n_keys < tq:
                p_buf[slot, n_keys:tq, cols] = jnp.zeros((tq - n_keys, tc), BF16)
            al_buf[slot, :, cols] = jnp.exp2(m_prev - m_new)
            m_sc[:, cols] = m_new

    def stage_values(u, slot):
        vt = vt_ref[:, key_rows(u, tq)]
        for cols in col_tiles:
            pv = jnp.dot(vt, p_buf[slot, :, cols], preferred_element_type=F32)
            acc_sc[:, cols] = al_buf[slot, :, cols] * acc_sc[:, cols] + pv

    def trip(i, slot):
        stage_scores(i, slot)
        stage_values(i - 2, slot)
        stage_softmax(1 - slot, False)

    def drain(last_slot):
        stage_values(n - 2, 1 - last_slot)
        stage_softmax(last_slot, False)
        stage_values(n - 1, last_slot)

    stage_scores_diagonal(0)

    @pl.when(n >= 2)
    def _():
        stage_softmax(0, True)
        stage_scores(1, 1)

    def body(ii, carry):
        i = 2 + 2 * ii
        trip(i, 0)
        trip(i + 1, 1)
        return carry

    lax.fori_loop(0, (n - 2) // 2, body, 0)
    n_odd = n % 2 == 1

    @pl.when(jnp.logical_and(n_odd, n >= 3))
    def _():
        trip(n - 1, 0)
        drain(0)

    @pl.when(jnp.logical_not(n_odd))
    def _():
        drain(1)

    @pl.when(n == 1)
    def _():
        stage_softmax(0, True)
        stage_values(0, 0)

    o = (acc_sc[0:MLA_V, :] / acc_sc[MLA_V:MLA_V + 1, :]).T
    o_ref[...] = (o * jax.nn.silu(g_ref[...])).astype(o_ref.dtype)


def _pair_table(nq):
    pairs = [(qi, qi - d) for d in range(1, nq) for qi in range(d, nq)] or [(0, 0)]
    return jnp.asarray(pairs, jnp.int32).T


def _mla_attn(qt, k, vt, h1):
    s = k.shape[0]
    tq = min(MLA_TQ, s)
    tc = min(ATT_TC, tq)
    ga_blk = H1_GA // LANE
    return pl.pallas_call(
        functools.partial(_mla_attn_kernel, tq=tq, tc=tc),
        grid=(MLA_HEADS, s // tq),
        in_specs=[pl.BlockSpec((MLA_QK_PAD, tq), lambda h, i: (h, i)),
                  pl.BlockSpec((s, MLA_QK_PAD), lambda h, i: (0, h)),
                  pl.BlockSpec((None, MLA_V_AUG, s), lambda h, i: (h, 0, 0)),
                  pl.BlockSpec((tq, LANE), lambda h, i: (i, ga_blk + h))],
        out_specs=pl.BlockSpec((tq, MLA_V), lambda h, i: (i, h)),
        out_shape=jax.ShapeDtypeStruct((s, MLA_WIDTH), BF16),
        scratch_shapes=[pltpu.VMEM((1, tq), F32), pltpu.VMEM((MLA_V_AUG, tq), F32),
                        pltpu.VMEM((2, tq, tq), F32), pltpu.VMEM((2, tq, tq), BF16), pltpu.VMEM((2, 1, tq), F32)],
        compiler_params=_cparams("parallel", "parallel", flags=ATT_FLAGS),
        name="mla_attn",
    )(qt, k, vt, h1)


def _sb_attn_kernel(tab_ref, qt_ref, k_ref, vt_ref, tri_ref, g_ref, o_ref, r_st, acc_st,
                    lb_buf, hi_buf, x_buf, cs_buf, a_buf, *, tq, tc, nq):
    n_sub = tq // tc
    n_off = nq * (nq - 1) // 2
    r_st[...] = jnp.zeros(r_st.shape, F32)
    acc_st[...] = jnp.zeros(acc_st.shape, F32)
    sub = [slice(t * tc, (t + 1) * tc) for t in range(n_sub)]
    strict = (lax.broadcasted_iota(jnp.int32, (tc, tc), 0)
              < lax.broadcasted_iota(jnp.int32, (tc, tc), 1))

    def q_cols(qi, c):
        return pl.ds(pl.multiple_of(qi * tq + c * tc, tc), tc)

    def stage_a(qi, j, slot, diag):
        for kk in range(n_sub):
            k = k_ref[pl.ds(pl.multiple_of(j * tq + kk * tc, tc), tc), :]
            for c in range(n_sub):
                if diag and kk > c:
                    hi_buf[slot, sub[kk], sub[c]] = jnp.zeros((tc, tc), BF16)
                    continue
                y = jnp.dot(k, qt_ref[:, q_cols(qi, c)], preferred_element_type=F32)
                neg_part = jnp.minimum(y, 0.0)
                neg_relu = neg_part - y
                t = jnp.log(1.0 + jnp.exp2(neg_part + neg_relu)) * LOG2E
                log_1mb = neg_relu - t
                if diag and kk == c:
                    log_1mb = jnp.where(strict, log_1mb, 0.0)
                lb_buf[slot, sub[kk], sub[c]] = neg_part - t
                hi_buf[slot, sub[kk], sub[c]] = log_1mb.astype(BF16)

    def stage_b(slot):
        tri = tri_ref[...]
        for kk in range(n_sub):
            for c in range(n_sub):
                r = jnp.dot(tri, hi_buf[slot, sub[kk], sub[c]], preferred_element_type=F32)
                x_buf[slot, sub[kk], sub[c]] = r[:tc] + lb_buf[slot, sub[kk], sub[c]]
                cs_buf[slot, kk, :, sub[c]] = r[tc:tc + 1]

    def stage_c_weights(qi, slot, diag):
        for c in range(n_sub):
            run = r_st[:, q_cols(qi, c)]
            for kk in range(n_sub - 1, -1, -1):
                if diag and kk > c:
                    a_buf[sub[kk], sub[c]] = jnp.zeros((tc, tc), BF16)
                    continue
                a = jnp.exp2(x_buf[slot, sub[kk], sub[c]] + run)
                if diag and kk == c:
                    a = jnp.where(strict, a, 0.0)
                a_buf[sub[kk], sub[c]] = a.astype(BF16)
                run = run + cs_buf[slot, kk, :, sub[c]]
            r_st[:, q_cols(qi, c)] = run

    def stage_c_values(qi, j):
        vt = vt_ref[:, pl.ds(pl.multiple_of(j * tq, tq), tq)]
        for c in range(n_sub):
            acc_st[:, q_cols(qi, c)] += jnp.dot(vt, a_buf[:, sub[c]], preferred_element_type=F32)

    _run_pipeline(nq,
                  lambda t, slot: stage_a(t, t, slot, True),
                  lambda t, slot: stage_b(slot),
                  lambda t, slot: stage_c_weights(t, slot, True),
                  lambda t, slot: stage_c_values(t, t))
    _run_pipeline(n_off,
                  lambda t, slot: stage_a(tab_ref[0, t], tab_ref[1, t], slot, False),
                  lambda t, slot: stage_b(slot),
                  lambda t, slot: stage_c_weights(tab_ref[0, t], slot, False),
                  lambda t, slot: stage_c_values(tab_ref[0, t], tab_ref[1, t]))

    def finish(qi, carry):
        rows = pl.ds(pl.multiple_of(qi * tq, tq), tq)
        o_ref[rows, :] = (acc_st[:, rows].T * jax.nn.silu(g_ref[rows, :])).astype(o_ref.dtype)
        return carry

    lax.fori_loop(0, nq, finish, 0)


def _sb_attn(h2, h2t, h1, tri):
    s = h2.shape[0]
    tq = min(SB_TQ, s)
    tc = tri.shape[1]
    nq = s // tq
    qb, kb, vb, gb = H2T_SBQ // LANE, H2_SBK // LANE, H2T_SBV // LANE, H1_GC // LANE
    return pl.pallas_call(
        functools.partial(_sb_attn_kernel, tq=tq, tc=tc, nq=nq),
        grid=(SB_HEADS,),
        in_specs=[pl.BlockSpec(memory_space=pltpu.SMEM),
                  pl.BlockSpec((LANE, s), lambda h: (qb + h, 0)),
                  pl.BlockSpec((s, LANE), lambda h: (0, kb + h)),
                  pl.BlockSpec((LANE, s), lambda h: (vb + h, 0)),
                  pl.BlockSpec(tri.shape, lambda h: (0, 0)),
                  pl.BlockSpec((s, LANE), lambda h: (0, gb + h))],
        out_specs=pl.BlockSpec((s, LANE), lambda h: (0, h)),
        out_shape=jax.ShapeDtypeStruct((s, SB_WIDTH), BF16),
        scratch_shapes=[pltpu.VMEM((1, s), F32), pltpu.VMEM((SB_HEAD_DIM, s), F32),
                        pltpu.VMEM((2, tq, tq), F32), pltpu.VMEM((2, tq, tq), BF16),
                        pltpu.VMEM((2, tq, tq), F32), pltpu.VMEM((2, tq // tc, 1, tq), F32),
                        pltpu.VMEM((tq, tq), BF16)],
        compiler_params=_cparams("parallel", flags=ATT_FLAGS),
        name="sb_attn",
    )(_pair_table(nq), h2t, h2, h2t, tri, h1)


def _gmlp_kernel(u_ref, v_ref, g_ref, lng_ref, lnb_ref, w_ref, b_ref, o_ref, *, n_chunks):
    gv = jax.nn.gelu(v_ref[...])
    mu = jnp.mean(gv, axis=-1, keepdims=True)
    vc = gv - mu
    var = jnp.mean(vc * vc, axis=-1, keepdims=True)
    vn = (vc * lax.rsqrt(var + LN_EPS) * lng_ref[...] + lnb_ref[...]).astype(BF16)
    t_chunk = lax.broadcasted_iota(jnp.int32, (SG_CHUNK, SG_CHUNK), 0) // CHUNK
    s_chunk = lax.broadcasted_iota(jnp.int32, (SG_CHUNK, SG_CHUNK), 1) // CHUNK
    mask = (s_chunk <= t_chunk).astype(F32)
    for g in range(SG_GROUPS):
        w_sp = (w_ref[g] * mask).astype(BF16)
        bias = b_ref[:, g:g + 1]
        cols = slice(g * SG_GROUP_CH, (g + 1) * SG_GROUP_CH)
        for n in range(n_chunks):
            rows = slice(n * SG_CHUNK, (n + 1) * SG_CHUNK)
            mixed = jnp.dot(w_sp, vn[rows, cols], preferred_element_type=F32) + bias
            o_b = jax.nn.gelu(u_ref[rows, cols]) * mixed
            o_ref[rows, cols] = (o_b * jax.nn.silu(g_ref[rows, cols])).astype(o_ref.dtype)


def _gmlp(h1, layer, lng, lnb, w, b_t):
    s = h1.shape[0]
    tm = min(s, 512)
    full = lambda a: pl.BlockSpec((None,) + a.shape[1:], lambda i: (layer,) + (0,) * (a.ndim - 1))
    return pl.pallas_call(
        functools.partial(_gmlp_kernel, n_chunks=tm // SG_CHUNK),
        grid=(s // tm,),
        in_specs=[pl.BlockSpec((tm, SG_WIDTH), lambda i: (i, H1_SGU // SG_WIDTH)),
                  pl.BlockSpec((tm, SG_WIDTH), lambda i: (i, H1_SGV // SG_WIDTH)),
                  pl.BlockSpec((tm, SG_WIDTH), lambda i: (i, H1_GB // SG_WIDTH)),
                  full(lng), full(lnb), full(w), full(b_t)],
        out_specs=pl.BlockSpec((tm, SG_WIDTH), lambda i: (i, 0)),
        out_shape=jax.ShapeDtypeStruct((s, SG_WIDTH), BF16),
        compiler_params=_cparams("parallel"),
        name="gmlp",
    )(h1, h1, h1, lng, lnb, w, b_t)


def _mem_kv_kernel(mem_ref, wk_ref, wv_ref, kbd_ref, vbd_ref):
    mem = mem_ref[...].astype(BF16)
    mk_t = jnp.dot(mem, wk_ref[...], preferred_element_type=F32).T
    mv = jnp.dot(mem, wv_ref[...], preferred_element_type=F32)
    feat_row = lax.broadcasted_iota(jnp.int32, mk_t.shape, 0) // MEM_HEAD_DIM
    feat_col = lax.broadcasted_iota(jnp.int32, mv.shape, 1) // MEM_HEAD_DIM
    for h in range(MEM_HEADS):
        seg = slice(h * MEM_TOKENS, (h + 1) * MEM_TOKENS)
        kbd_ref[:, seg] = jnp.where(feat_row == h, mk_t, 0.0).astype(BF16)
        vbd_ref[seg, :] = jnp.where(feat_col == h, mv, 0.0).astype(BF16)


def _mem_kv(mem, layer, wk, wv):
    nt = MEM_HEADS * MEM_TOKENS
    full2 = lambda shape: pl.BlockSpec(shape, lambda i: (0, 0))
    stack = lambda a: pl.BlockSpec((None,) + a.shape[1:], lambda i: (layer, 0, 0))
    return pl.pallas_call(
        _mem_kv_kernel,
        grid=(1,),
        in_specs=[full2(mem.shape), stack(wk), stack(wv)],
        out_specs=[full2((MEM_WIDTH, nt)), full2((nt, MEM_WIDTH))],
        out_shape=[jax.ShapeDtypeStruct((MEM_WIDTH, nt), BF16),
                   jax.ShapeDtypeStruct((nt, MEM_WIDTH), BF16)],
        compiler_params=_cparams("arbitrary"),
        name="mem_kv",
    )(mem, wk, wv)


def _mem_attn_kernel(q_ref, kbd_ref, vbd_ref, g_ref, o_ref):
    logits = jnp.dot(q_ref[...], kbd_ref[...], preferred_element_type=F32) * MEM_SCALE
    probs = []
    for h in range(MEM_HEADS):
        seg = logits[:, h * MEM_TOKENS:(h + 1) * MEM_TOKENS]
        e = jnp.exp(seg - jnp.max(seg, axis=-1, keepdims=True))
        probs.append((e / jnp.sum(e, axis=-1, keepdims=True)).astype(BF16))
    p = jnp.concatenate(probs, axis=-1)
    o = jnp.dot(p, vbd_ref[...], preferred_element_type=F32)
    o_ref[...] = (o * jax.nn.silu(g_ref[...])).astype(o_ref.dtype)


def _mem_attn(h2, kbd, vbd, h1):
    s = h2.shape[0]
    tm = min(s, 512)
    full2 = lambda shape: pl.BlockSpec(shape, lambda i: (0, 0))
    return pl.pallas_call(
        _mem_attn_kernel,
        grid=(s // tm,),
        in_specs=[pl.BlockSpec((tm, MEM_WIDTH), lambda i: (i, H2_MQ // MEM_WIDTH)),
                  full2(kbd.shape), full2(vbd.shape),
                  pl.BlockSpec((tm, MEM_WIDTH), lambda i: (i, H1_GM // MEM_WIDTH))],
        out_specs=pl.BlockSpec((tm, MEM_WIDTH), lambda i: (i, 0)),
        out_shape=jax.ShapeDtypeStruct((s, MEM_WIDTH), BF16),
        compiler_params=_cparams("parallel"),
        name="mem_attn",
    )(h2, kbd, vbd, h1)


def _out_ln_kernel(ya_ref, yb_ref, yc_ref, ym_ref, w_ref, x_ref, g_ref, b_ref, o_ref, obf_ref, *, alpha):
    tm = x_ref.shape[0]
    for rows in (slice(0, tm // 2), slice(tm // 2, tm)):
        y = None
        row = 0
        for y_ref in (ya_ref, yb_ref, yc_ref, ym_ref):
            width = y_ref.shape[1]
            part = jnp.dot(y_ref[rows, :], w_ref[row:row + width, :], preferred_element_type=F32)
            y = part if y is None else y + part
            row += width
        r = alpha * x_ref[rows, :] + y
        mu = jnp.mean(r, axis=-1, keepdims=True)
        rc = r - mu
        var = jnp.mean(rc * rc, axis=-1, keepdims=True)
        out = rc * lax.rsqrt(var + LN_EPS) * g_ref[...] + b_ref[...]
        o_ref[rows, :] = out
        obf_ref[rows, :] = out.astype(BF16)


def _out_ln(ya, yb, yc, ym, w, layer, x, g, b, alpha):
    s, d = x.shape
    tm = min(s, 512)
    row = lambda w: pl.BlockSpec((tm, w), lambda i: (i, 0))
    full = lambda a: pl.BlockSpec((None,) + a.shape[1:], lambda i: (layer, 0, 0))
    return pl.pallas_call(
        functools.partial(_out_ln_kernel, alpha=alpha),
        grid=(s // tm,),
        in_specs=[row(MLA_WIDTH), row(SG_WIDTH), row(SB_WIDTH), row(MEM_WIDTH),
                  full(w), row(d), full(g), full(b)],
        out_specs=[row(d), row(d)],
        out_shape=[jax.ShapeDtypeStruct((s, d), F32), jax.ShapeDtypeStruct((s, d), BF16)],
        compiler_params=_cparams("parallel"),
        name="out_ln",
    )(ya, yb, yc, ym, w, x, g, b)


def _prep_mla_weights(w_uq, w_ukv):
    depth = w_uq.shape[0]
    wq = w_uq.reshape(depth, MLA_Q_RANK, MLA_HEADS, MLA_NOPE + MLA_ROPE)
    wq = jnp.pad(wq, ((0, 0), (0, 0), (0, 0), (0, MLA_QK_PAD - MLA_NOPE - MLA_ROPE)))
    wqt = jnp.swapaxes(wq.reshape(depth, MLA_Q_RANK, MLA_HEADS * MLA_QK_PAD), 1, 2).astype(BF16)
    wkv = w_ukv.reshape(depth, MLA_KV_RANK, MLA_HEADS, MLA_NOPE + MLA_V)
    wk = wkv[..., :MLA_NOPE].reshape(depth, MLA_KV_RANK, MLA_WIDTH).astype(BF16)
    wvt = jnp.swapaxes(wkv[..., MLA_NOPE:].reshape(depth, MLA_KV_RANK, MLA_WIDTH), 1, 2).astype(BF16)
    return wqt, wk, wvt


def kernel(x, mem, positions, w_in, q_norm_g, w_uq, kv_norm_g, w_ukv, sg_ln_g, sg_ln_b, sg_w, sg_b,
           w_mem_k, w_mem_v, w_out, ln_g, ln_b):
    b, s, d = x.shape
    depth = w_in.shape[0]
    alpha = (2.0 * depth) ** 0.25
    sb_tc = min(s, ATT_TC)

    inv_freq = ROPE_THETA ** (-jnp.arange(0, MLA_ROPE, 2, dtype=F32) / MLA_ROPE)
    invf = jnp.concatenate([inv_freq, inv_freq, jnp.zeros((LANE - MLA_ROPE,), F32)])
    tri = jnp.concatenate(
        [(lax.broadcasted_iota(jnp.int32, (sb_tc, sb_tc), 1)
          > lax.broadcasted_iota(jnp.int32, (sb_tc, sb_tc), 0)).astype(BF16),
         jnp.ones((BF16_ROWS, sb_tc), BF16)], axis=0)
    w1, w2, w2q = _w_in_prep(jnp.swapaxes(w_in, 1, 2))
    wqt, wk, wvt = _prep_mla_weights(w_uq, w_ukv)
    wmk, wmv, wo = w_mem_k.astype(BF16), w_mem_v.astype(BF16), w_out.astype(BF16)
    row3 = lambda p: p[:, None, :]
    sg_b_t = jnp.swapaxes(sg_b, 1, 2)
    h2t_scale = jnp.concatenate([jnp.full((SB_WIDTH, 1), SB_SCALE * LOG2E, F32),
                                 jnp.ones((SB_WIDTH, 1), F32)], axis=0)

    outs = []
    for bi in range(b):
        xf = x[bi]
        xb = xf.astype(BF16)
        cos, sin, cos_t, sin_t = _rope_tables(positions[bi].reshape(s, 1), positions[bi].reshape(1, s),
                                              invf[None, :], invf[:, None])
        for l in range(depth):
            h1 = _matmul(xb, w1, l, F32, 1024, 1024, "in_proj_f32")
            h2 = _matmul(xb, w2, l, BF16, 1024, H2_W, "in_proj_bf16")
            h2t = _matmul_nt(w2q, l, xb, h2t_scale, BF16, 1024, H2T_W, "in_proj_bf16_t")
            qt, k, vt = _mla_prep(h1, l, row3(q_norm_g), row3(kv_norm_g), wqt, wk, wvt, cos, sin, cos_t, sin_t)
            ya = _mla_attn(qt, k, vt, h1)
            yc = _sb_attn(h2, h2t, h1, tri)
            yb = _gmlp(h1, l, row3(sg_ln_g), row3(sg_ln_b), sg_w, sg_b_t)
            kbd, vbd = _mem_kv(mem[bi], l, wmk, wmv)
            ym = _mem_attn(h2, kbd, vbd, h1)
            xf, xb = _out_ln(ya, yb, yc, ym, wo, l, xf, row3(ln_g), row3(ln_b), alpha)
        outs.append(xf)
    return outs[0][None] if b == 1 else jnp.stack(outs, axis=0)
```

```python
import functools
import math

import jax
import jax.numpy as jnp
from jax import lax
from jax.experimental import pallas as pl
from jax.experimental.pallas import tpu as pltpu

F32 = jnp.float32
BF16 = jnp.bfloat16

D_MODEL = 2048
CHUNK = 64
MLA_HEADS = 6
MLA_NOPE = 128
MLA_ROPE = 64
MLA_V = 128
MLA_Q_RANK = 512
MLA_KV_RANK = 256
MLA_WIDTH = MLA_HEADS * MLA_V
ROPE_THETA = 10000.0
SG_GROUPS = 4
SG_GROUP_CH = 128
SG_WIDTH = SG_GROUPS * SG_GROUP_CH
SG_CHUNK = 128
SB_HEADS = 4
SB_HEAD_DIM = 128
SB_WIDTH = SB_HEADS * SB_HEAD_DIM
MEM_TOKENS = 256
MEM_HEADS = 4
MEM_HEAD_DIM = 64
MEM_WIDTH = MEM_HEADS * MEM_HEAD_DIM
LN_EPS = 1e-5
RMS_EPS = 1e-6

LOG2E = math.log2(math.e)
MLA_SCALE = 1.0 / math.sqrt(MLA_NOPE + MLA_ROPE)
SB_SCALE = 1.0 / math.sqrt(SB_HEAD_DIM)
MEM_SCALE = 1.0 / math.sqrt(MEM_HEAD_DIM)

LANE = 128
BF16_ROWS = 16
MXU_TILE = 256
MLA_QK_PAD = 256
MLA_V_AUG = MLA_V + BF16_ROWS
NEG_BIG = -1e30
VMEM_LIMIT = 48 * 1024 * 1024
MLA_TQ = 1024
SB_TQ = 512
ATT_TC = MXU_TILE

H1_LAT_W = 1024
H1_SGU = 1024
H1_SGV = 1536
H1_GB = 2048
H1_GA = 2560
H1_GC = 3328
H1_GM = 3840
H1_W = 4096
H2_SBK = 0
H2_MQ = 512
H2_W = 768
H2T_SBQ = 0
H2T_SBV = 512
H2T_W = 1024

_NT = (((1,), (1,)), ((), ()))


def _cparams(*sem, flags=None):
    return pltpu.CompilerParams(dimension_semantics=sem, vmem_limit_bytes=VMEM_LIMIT, flags=flags)


ATT_FLAGS = None


def _rope_table_kernel(pos_col_ref, pos_row_ref, invf_row_ref, invf_col_ref, cos_ref, sin_ref, cos_t_ref, sin_t_ref):
    ang = pos_col_ref[...].astype(F32) * invf_row_ref[...]
    cos_ref[...] = jnp.cos(ang)
    sin_ref[...] = jnp.sin(ang)
    ang_t = invf_col_ref[...] * pos_row_ref[...].astype(F32)
    cos_t_ref[...] = jnp.cos(ang_t)
    sin_t_ref[...] = jnp.sin(ang_t)


def _rope_tables(pos_col, pos_row, invf_row, invf_col):
    s = pos_col.shape[0]
    tm = min(s, 1024)
    return pl.pallas_call(
        _rope_table_kernel,
        grid=(s // tm,),
        in_specs=[pl.BlockSpec((tm, 1), lambda i: (i, 0)),
                  pl.BlockSpec((1, tm), lambda i: (0, i)),
                  pl.BlockSpec((1, LANE), lambda i: (0, 0)),
                  pl.BlockSpec((LANE, 1), lambda i: (0, 0))],
        out_specs=[pl.BlockSpec((tm, LANE), lambda i: (i, 0)),
                   pl.BlockSpec((tm, LANE), lambda i: (i, 0)),
                   pl.BlockSpec((LANE, tm), lambda i: (0, i)),
                   pl.BlockSpec((LANE, tm), lambda i: (0, i))],
        out_shape=[jax.ShapeDtypeStruct((s, LANE), F32)] * 2 + [jax.ShapeDtypeStruct((LANE, s), F32)] * 2,
        compiler_params=_cparams("parallel"),
        name="rope_tables",
    )(pos_col, pos_row, invf_row, invf_col)


_IN_SEGMENTS = ("c_q", "c_kv", "k_pe", "g_a", "sg_u", "sg_v", "g_b", "sb_q", "sb_k", "sb_v", "g_c", "m_q", "g_m")
_IN_WIDTHS = (MLA_Q_RANK, MLA_KV_RANK, MLA_ROPE, MLA_WIDTH, SG_WIDTH, SG_WIDTH, SG_WIDTH,
              SB_WIDTH, SB_WIDTH, SB_WIDTH, SB_WIDTH, MEM_WIDTH, MEM_WIDTH)
_IN_OFFSETS = tuple(sum(_IN_WIDTHS[:i]) for i in range(len(_IN_WIDTHS) + 1))
IN_ROWS = {name: (_IN_OFFSETS[i], _IN_OFFSETS[i + 1]) for i, name in enumerate(_IN_SEGMENTS)}


def _w_in_prep_kernel(wt_ref, w1_ref, w2_ref, w2q_ref):
    def put(dst, off, name):
        a, b = IN_ROWS[name]
        dst[off:off + b - a, :] = wt_ref[a:b, :].astype(BF16)

    lat_end = MLA_Q_RANK + MLA_KV_RANK + MLA_ROPE
    for name, off in (("c_q", 0), ("c_kv", MLA_Q_RANK), ("k_pe", MLA_Q_RANK + MLA_KV_RANK),
                      ("sg_u", H1_SGU), ("sg_v", H1_SGV), ("g_b", H1_GB), ("g_a", H1_GA), ("g_c", H1_GC),
                      ("g_m", H1_GM)):
        put(w1_ref, off, name)
    w1_ref[lat_end:H1_LAT_W, :] = jnp.zeros((H1_LAT_W - lat_end, w1_ref.shape[1]), BF16)
    put(w2_ref, H2_SBK, "sb_k")
    put(w2_ref, H2_MQ, "m_q")
    put(w2q_ref, H2T_SBQ, "sb_q")
    put(w2q_ref, H2T_SBV, "sb_v")


def _w_in_prep(w_in_t):
    depth, n, k = w_in_t.shape
    tk = 256
    blk = lambda rows: pl.BlockSpec((None, rows, tk), lambda l, i: (l, 0, i))
    return pl.pallas_call(
        _w_in_prep_kernel,
        grid=(depth, k // tk),
        in_specs=[blk(n)],
        out_specs=[blk(H1_W), blk(H2_W), blk(H2T_W)],
        out_shape=[jax.ShapeDtypeStruct((depth, H1_W, k), BF16),
                   jax.ShapeDtypeStruct((depth, H2_W, k), BF16),
                   jax.ShapeDtypeStruct((depth, H2T_W, k), BF16)],
        compiler_params=_cparams("parallel", "parallel"),
        name="w_in_prep",
    )(w_in_t)


def _mm_kernel(x_ref, wt_ref, o_ref):
    x = x_ref[...].astype(BF16)
    o_ref[...] = lax.dot_general(x, wt_ref[...], _NT, preferred_element_type=F32).astype(o_ref.dtype)


def _matmul(x, wt, layer, out_dtype, tm, tn, name):
    m, k = x.shape
    n = wt.shape[1]
    tm = min(tm, m)
    return pl.pallas_call(
        _mm_kernel,
        grid=(m // tm, n // tn),
        in_specs=[pl.BlockSpec((tm, k), lambda i, j: (i, 0)),
                  pl.BlockSpec((None, tn, k), lambda i, j: (layer, j, 0))],
        out_specs=pl.BlockSpec((tm, tn), lambda i, j: (i, j)),
        out_shape=jax.ShapeDtypeStruct((m, n), out_dtype),
        compiler_params=_cparams("parallel", "parallel"),
        name=name,
    )(x, wt)


def _mm_nt_kernel(wt_ref, x_ref, scale_ref, o_ref):
    x = x_ref[...].astype(BF16)
    acc = lax.dot_general(wt_ref[...], x, _NT, preferred_element_type=F32)
    o_ref[...] = (acc * scale_ref[...]).astype(o_ref.dtype)


def _matmul_nt(wt, layer, x, row_scale, out_dtype, tm, tn, name):
    n, k = wt.shape[1:]
    m = x.shape[0]
    tm = min(tm, m)
    return pl.pallas_call(
        _mm_nt_kernel,
        grid=(m // tm, n // tn),
        in_specs=[pl.BlockSpec((None, tn, k), lambda i, j: (layer, j, 0)),
                  pl.BlockSpec((tm, k), lambda i, j: (i, 0)),
                  pl.BlockSpec((tn, 1), lambda i, j: (j, 0))],
        out_specs=pl.BlockSpec((tn, tm), lambda i, j: (j, i)),
        out_shape=jax.ShapeDtypeStruct((n, m), out_dtype),
        compiler_params=_cparams("parallel", "parallel"),
        name=name,
    )(wt, x, row_scale)


def _rms(x, g):
    ms = jnp.mean(x * x, axis=-1, keepdims=True)
    return x * lax.rsqrt(ms + RMS_EPS) * g


def _mla_prep_kernel(lat_ref, qg_ref, kvg_ref, wqt_ref, wk_ref, wvt_ref, cos_ref, sin_ref, cos_t_ref, sin_t_ref,
                     qt_ref, k_ref, vt_ref):
    half = MLA_ROPE // 2
    cqn = _rms(lat_ref[:, 0:MLA_Q_RANK], qg_ref[...]).astype(BF16)
    ckvn = _rms(lat_ref[:, MLA_Q_RANK:MLA_Q_RANK + MLA_KV_RANK], kvg_ref[...]).astype(BF16)
    qa_t = lax.dot_general(wqt_ref[...], cqn, _NT, preferred_element_type=F32) * (MLA_SCALE * LOG2E)
    kn = jnp.dot(ckvn, wk_ref[...], preferred_element_type=F32)
    v_t = lax.dot_general(wvt_ref[...], ckvn, _NT, preferred_element_type=F32).astype(BF16)
    ones = jnp.ones((BF16_ROWS, v_t.shape[1]), BF16)
    cos_t = cos_t_ref[...]
    sin_t = sin_t_ref[...]
    kpe = lat_ref[:, MLA_Q_RANK + MLA_KV_RANK:MLA_Q_RANK + MLA_KV_RANK + LANE]
    lane = lax.broadcasted_iota(jnp.int32, kpe.shape, 1)
    kpe_rot = jnp.where(lane < half, -pltpu.roll(kpe, LANE - half, 1),
                        jnp.where(lane < MLA_ROPE, pltpu.roll(kpe, half, 1), 0.0))
    krot = (kpe * cos_ref[...] + kpe_rot * sin_ref[...]).astype(BF16)
    for h in range(MLA_HEADS):
        c0 = h * MLA_QK_PAD
        qt_ref[c0:c0 + LANE, :] = qa_t[c0:c0 + LANE, :].astype(BF16)
        qr = qa_t[c0 + LANE:c0 + 2 * LANE, :]
        qr_rot = jnp.concatenate([-qr[half:MLA_ROPE], qr[0:half], qr[MLA_ROPE:LANE]], axis=0)
        qt_ref[c0 + LANE:c0 + 2 * LANE, :] = (qr * cos_t + qr_rot * sin_t).astype(BF16)
        k_ref[:, c0:c0 + LANE] = kn[:, h * LANE:(h + 1) * LANE].astype(BF16)
        k_ref[:, c0 + LANE:c0 + 2 * LANE] = krot
        vt_ref[h, 0:MLA_V, :] = v_t[h * MLA_V:(h + 1) * MLA_V, :]
        vt_ref[h, MLA_V:MLA_V_AUG, :] = ones


def _mla_prep(h1, layer, qg, kvg, wqt, wk, wvt, cos, sin, cos_t, sin_t):
    s = h1.shape[0]
    tm = min(s, 512)
    nq = MLA_HEADS * MLA_QK_PAD
    full = lambda a: pl.BlockSpec((None,) + a.shape[1:], lambda i: (layer, 0, 0))
    rows = lambda w: pl.BlockSpec((tm, w), lambda i: (i, 0))
    cols = lambda w: pl.BlockSpec((w, tm), lambda i: (0, i))
    return pl.pallas_call(
        _mla_prep_kernel,
        grid=(s // tm,),
        in_specs=[rows(H1_LAT_W), full(qg), full(kvg), full(wqt), full(wk), full(wvt),
                  rows(LANE), rows(LANE), cols(LANE), cols(LANE)],
        out_specs=[cols(nq), rows(nq), pl.BlockSpec((MLA_HEADS, MLA_V_AUG, tm), lambda i: (0, 0, i))],
        out_shape=[jax.ShapeDtypeStruct((nq, s), BF16),
                   jax.ShapeDtypeStruct((s, nq), BF16),
                   jax.ShapeDtypeStruct((MLA_HEADS, MLA_V_AUG, s), BF16)],
        compiler_params=_cparams("parallel"),
        name="mla_prep",
    )(h1, qg, kvg, wqt, wk, wvt, cos, sin, cos_t, sin_t)


def _run_pipeline(n_items, stage_a, stage_b, stage_c_weights, stage_c_values):
    if n_items == 0:
        return

    def stage_c(t, slot):
        stage_c_weights(t, slot)
        stage_c_values(t, slot)

    def trip(i, slot):
        stage_c_weights(i - 2, slot)
        stage_a(i, slot)
        stage_c_values(i - 2, slot)
        stage_b(i - 1, 1 - slot)

    stage_a(0, 0)
    if n_items >= 2:
        stage_a(1, 1)
    stage_b(0, 0)
    steady = max(n_items - 2, 0)

    def body(ii, carry):
        i = 2 + 2 * ii
        trip(i, 0)
        trip(i + 1, 1)
        return carry

    lax.fori_loop(0, steady // 2, body, 0)
    if steady % 2:
        trip(n_items - 1, (n_items - 1) % 2)
    if n_items >= 2:
        stage_c_weights(n_items - 2, n_items % 2)
        stage_b(n_items - 1, (n_items - 1) % 2)
        stage_c_values(n_items - 2, n_items % 2)
    stage_c(n_items - 1, (n_items - 1) % 2)


def _mla_attn_kernel(qt_ref, k_ref, vt_ref, g_ref, o_ref, m_sc, acc_sc, s_buf, p_buf, al_buf, *, tq, tc):
    n = pl.program_id(1) + 1
    m_sc[...] = jnp.full(m_sc.shape, NEG_BIG, F32)
    acc_sc[...] = jnp.zeros(acc_sc.shape, F32)
    col_tiles = [slice(c * tc, (c + 1) * tc) for c in range(tq // tc)]

    def key_rows(u, n_keys):
        j = jnp.where(u == 0, n - 1, u - 1)
        return pl.ds(pl.multiple_of(j * tq, tq), n_keys)

    def stage_scores(u, slot):
        k = k_ref[key_rows(u, tq), :]
        for cols in col_tiles:
            s_buf[slot, :, cols] = jnp.dot(k, qt_ref[:, cols], preferred_element_type=F32)

    def stage_scores_diagonal(slot):
        for c, cols in enumerate(col_tiles):
            n_keys = (c + 1) * tc
            k = k_ref[key_rows(0, n_keys), :]
            s_buf[slot, 0:n_keys, cols] = jnp.dot(k, qt_ref[:, cols], preferred_element_type=F32)

    def stage_softmax(slot, masked):
        for c, cols in enumerate(col_tiles):
            n_keys = (c + 1) * tc if masked else tq
            s = s_buf[slot, 0:n_keys, cols]
            if masked:
                kc = lax.broadcasted_iota(jnp.int32, s.shape, 0) // CHUNK
                qc = (lax.broadcasted_iota(jnp.int32, s.shape, 1) + c * tc) // CHUNK
                s = jnp.where(kc <= qc, s, NEG_BIG)
            m_prev = m_sc[:, cols]
            m_new = jnp.maximum(m_prev, jnp.max(s, axis=0, keepdims=True))
            p_buf[slot, 0:n_keys, cols] = jnp.exp2(s - m_new).astype(BF16)
            if n_keys < tq:
                p_buf[slot, n_keys:tq, cols] = jnp.zeros((tq - n_keys, tc), BF16)
            al_buf[slot, :, cols] = jnp.exp2(m_prev - m_new)
            m_sc[:, cols] = m_new

    def stage_values(u, slot):
        vt = vt_ref[:, key_rows(u, tq)]
        for cols in col_tiles:
            pv = jnp.dot(vt, p_buf[slot, :, cols], preferred_element_type=F32)
            acc_sc[:, cols] = al_buf[slot, :, cols] * acc_sc[:, cols] + pv

    def trip(i, slot):
        stage_scores(i, slot)
        stage_values(i - 2, slot)
        stage_softmax(1 - slot, False)

    def drain(last_slot):
        stage_values(n - 2, 1 - last_slot)
        stage_softmax(last_slot, False)
        stage_values(n - 1, last_slot)

    stage_scores_diagonal(0)

    @pl.when(n >= 2)
    def _():
        stage_softmax(0, True)
        stage_scores(1, 1)

    def body(ii, carry):
        i = 2 + 2 * ii
        trip(i, 0)
        trip(i + 1, 1)
        return carry

    lax.fori_loop(0, (n - 2) // 2, body, 0)
    n_odd = n % 2 == 1

    @pl.when(jnp.logical_and(n_odd, n >= 3))
    def _():
        trip(n - 1, 0)
        drain(0)

    @pl.when(jnp.logical_not(n_odd))
    def _():
        drain(1)

    @pl.when(n == 1)
    def _():
        stage_softmax(0, True)
        stage_values(0, 0)

    o = (acc_sc[0:MLA_V, :] / acc_sc[MLA_V:MLA_V + 1, :]).T
    o_ref[...] = (o * jax.nn.silu(g_ref[...])).astype(o_ref.dtype)


def _pair_table(nq):
    pairs = [(qi, qi - d) for d in range(1, nq) for qi in range(d, nq)] or [(0, 0)]
    return jnp.asarray(pairs, jnp.int32).T


def _mla_attn(qt, k, vt, h1):
    s = k.shape[0]
    tq = min(MLA_TQ, s)
    tc = min(ATT_TC, tq)
    ga_blk = H1_GA // LANE
    return pl.pallas_call(
        functools.partial(_mla_attn_kernel, tq=tq, tc=tc),
        grid=(MLA_HEADS, s // tq),
        in_specs=[pl.BlockSpec((MLA_QK_PAD, tq), lambda h, i: (h, i)),
                  pl.BlockSpec((s, MLA_QK_PAD), lambda h, i: (0, h)),
                  pl.BlockSpec((None, MLA_V_AUG, s), lambda h, i: (h, 0, 0)),
                  pl.BlockSpec((tq, LANE), lambda h, i: (i, ga_blk + h))],
        out_specs=pl.BlockSpec((tq, MLA_V), lambda h, i: (i, h)),
        out_shape=jax.ShapeDtypeStruct((s, MLA_WIDTH), BF16),
        scratch_shapes=[pltpu.VMEM((1, tq), F32), pltpu.VMEM((MLA_V_AUG, tq), F32),
                        pltpu.VMEM((2, tq, tq), F32), pltpu.VMEM((2, tq, tq), BF16), pltpu.VMEM((2, 1, tq), F32)],
        compiler_params=_cparams("parallel", "parallel", flags=ATT_FLAGS),
        name="mla_attn",
    )(qt, k, vt, h1)


def _sb_attn_kernel(tab_ref, qt_ref, k_ref, vt_ref, tri_ref, g_ref, o_ref, r_st, acc_st,
                    lb_buf, hi_buf, x_buf, cs_buf, a_buf, *, tq, tc, nq):
    n_sub = tq // tc
    n_off = nq * (nq - 1) // 2
    r_st[...] = jnp.zeros(r_st.shape, F32)
    acc_st[...] = jnp.zeros(acc_st.shape, F32)
    sub = [slice(t * tc, (t + 1) * tc) for t in range(n_sub)]
    strict = (lax.broadcasted_iota(jnp.int32, (tc, tc), 0)
              < lax.broadcasted_iota(jnp.int32, (tc, tc), 1))

    def q_cols(qi, c):
        return pl.ds(pl.multiple_of(qi * tq + c * tc, tc), tc)

    def stage_a(qi, j, slot, diag):
        for kk in range(n_sub):
            k = k_ref[pl.ds(pl.multiple_of(j * tq + kk * tc, tc), tc), :]
            for c in range(n_sub):
                if diag and kk > c:
                    hi_buf[slot, sub[kk], sub[c]] = jnp.zeros((tc, tc), BF16)
                    continue
                y = jnp.dot(k, qt_ref[:, q_cols(qi, c)], preferred_element_type=F32)
                neg_part = jnp.minimum(y, 0.0)
                neg_relu = neg_part - y
                t = jnp.log(1.0 + jnp.exp2(neg_part + neg_relu)) * LOG2E
                log_1mb = neg_relu - t
                if diag and kk == c:
                    log_1mb = jnp.where(strict, log_1mb, 0.0)
                lb_buf[slot, sub[kk], sub[c]] = neg_part - t
                hi_buf[slot, sub[kk], sub[c]] = log_1mb.astype(BF16)

    def stage_b(slot):
        tri = tri_ref[...]
        for kk in range(n_sub):
            for c in range(n_sub):
                r = jnp.dot(tri, hi_buf[slot, sub[kk], sub[c]], preferred_element_type=F32)
                x_buf[slot, sub[kk], sub[c]] = r[:tc] + lb_buf[slot, sub[kk], sub[c]]
                cs_buf[slot, kk, :, sub[c]] = r[tc:tc + 1]

    def stage_c_weights(qi, slot, diag):
        for c in range(n_sub):
            run = r_st[:, q_cols(qi, c)]
            for kk in range(n_sub - 1, -1, -1):
                if diag and kk > c:
                    a_buf[sub[kk], sub[c]] = jnp.zeros((tc, tc), BF16)
                    continue
                a = jnp.exp2(x_buf[slot, sub[kk], sub[c]] + run)
                if diag and kk == c:
                    a = jnp.where(strict, a, 0.0)
                a_buf[sub[kk], sub[c]] = a.astype(BF16)
                run = run + cs_buf[slot, kk, :, sub[c]]
            r_st[:, q_cols(qi, c)] = run

    def stage_c_values(qi, j):
        vt = vt_ref[:, pl.ds(pl.multiple_of(j * tq, tq), tq)]
        for c in range(n_sub):
            acc_st[:, q_cols(qi, c)] += jnp.dot(vt, a_buf[:, sub[c]], preferred_element_type=F32)

    _run_pipeline(nq,
                  lambda t, slot: stage_a(t, t, slot, True),
                  lambda t, slot: stage_b(slot),
                  lambda t, slot: stage_c_weights(t, slot, True),
                  lambda t, slot: stage_c_values(t, t))
    _run_pipeline(n_off,
                  lambda t, slot: stage_a(tab_ref[0, t], tab_ref[1, t], slot, False),
                  lambda t, slot: stage_b(slot),
                  lambda t, slot: stage_c_weights(tab_ref[0, t], slot, False),
                  lambda t, slot: stage_c_values(tab_ref[0, t], tab_ref[1, t]))

    def finish(qi, carry):
        rows = pl.ds(pl.multiple_of(qi * tq, tq), tq)
        o_ref[rows, :] = (acc_st[:, rows].T * jax.nn.silu(g_ref[rows, :])).astype(o_ref.dtype)
        return carry

    lax.fori_loop(0, nq, finish, 0)


def _sb_attn(h2, h2t, h1, tri):
    s = h2.shape[0]
    tq = min(SB_TQ, s)
    tc = tri.shape[1]
    nq = s // tq
    qb, kb, vb, gb = H2T_SBQ // LANE, H2_SBK // LANE, H2T_SBV // LANE, H1_GC // LANE
    return pl.pallas_call(
        functools.partial(_sb_attn_kernel, tq=tq, tc=tc, nq=nq),
        grid=(SB_HEADS,),
        in_specs=[pl.BlockSpec(memory_space=pltpu.SMEM),
                  pl.BlockSpec((LANE, s), lambda h: (qb + h, 0)),
                  pl.BlockSpec((s, LANE), lambda h: (0, kb + h)),
                  pl.BlockSpec((LANE, s), lambda h: (vb + h, 0)),
                  pl.BlockSpec(tri.shape, lambda h: (0, 0)),
                  pl.BlockSpec((s, LANE), lambda h: (0, gb + h))],
        out_specs=pl.BlockSpec((s, LANE), lambda h: (0, h)),
        out_shape=jax.ShapeDtypeStruct((s, SB_WIDTH), BF16),
        scratch_shapes=[pltpu.VMEM((1, s), F32), pltpu.VMEM((SB_HEAD_DIM, s), F32),
                        pltpu.VMEM((2, tq, tq), F32), pltpu.VMEM((2, tq, tq), BF16),
                        pltpu.VMEM((2, tq, tq), F32), pltpu.VMEM((2, tq // tc, 1, tq), F32),
                        pltpu.VMEM((tq, tq), BF16)],
        compiler_params=_cparams("parallel", flags=ATT_FLAGS),
        name="sb_attn",
    )(_pair_table(nq), h2t, h2, h2t, tri, h1)


def _gmlp_kernel(u_ref, v_ref, g_ref, lng_ref, lnb_ref, w_ref, b_ref, o_ref, *, n_chunks):
    gv = jax.nn.gelu(v_ref[...])
    mu = jnp.mean(gv, axis=-1, keepdims=True)
    vc = gv - mu
    var = jnp.mean(vc * vc, axis=-1, keepdims=True)
    vn = (vc * lax.rsqrt(var + LN_EPS) * lng_ref[...] + lnb_ref[...]).astype(BF16)
    t_chunk = lax.broadcasted_iota(jnp.int32, (SG_CHUNK, SG_CHUNK), 0) // CHUNK
    s_chunk = lax.broadcasted_iota(jnp.int32, (SG_CHUNK, SG_CHUNK), 1) // CHUNK
    mask = (s_chunk <= t_chunk).astype(F32)
    for g in range(SG_GROUPS):
        w_sp = (w_ref[g] * mask).astype(BF16)
        bias = b_ref[:, g:g + 1]
        cols = slice(g * SG_GROUP_CH, (g + 1) * SG_GROUP_CH)
        for n in range(n_chunks):
            rows = slice(n * SG_CHUNK, (n + 1) * SG_CHUNK)
            mixed = jnp.dot(w_sp, vn[rows, cols], preferred_element_type=F32) + bias
            o_b = jax.nn.gelu(u_ref[rows, cols]) * mixed
            o_ref[rows, cols] = (o_b * jax.nn.silu(g_ref[rows, cols])).astype(o_ref.dtype)


def _gmlp(h1, layer, lng, lnb, w, b_t):
    s = h1.shape[0]
    tm = min(s, 512)
    full = lambda a: pl.BlockSpec((None,) + a.shape[1:], lambda i: (layer,) + (0,) * (a.ndim - 1))
    return pl.pallas_call(
        functools.partial(_gmlp_kernel, n_chunks=tm // SG_CHUNK),
        grid=(s // tm,),
        in_specs=[pl.BlockSpec((tm, SG_WIDTH), lambda i: (i, H1_SGU // SG_WIDTH)),
                  pl.BlockSpec((tm, SG_WIDTH), lambda i: (i, H1_SGV // SG_WIDTH)),
                  pl.BlockSpec((tm, SG_WIDTH), lambda i: (i, H1_GB // SG_WIDTH)),
                  full(lng), full(lnb), full(w), full(b_t)],
        out_specs=pl.BlockSpec((tm, SG_WIDTH), lambda i: (i, 0)),
        out_shape=jax.ShapeDtypeStruct((s, SG_WIDTH), BF16),
        compiler_params=_cparams("parallel"),
        name="gmlp",
    )(h1, h1, h1, lng, lnb, w, b_t)


def _mem_kv_kernel(mem_ref, wk_ref, wv_ref, kbd_ref, vbd_ref):
    mem = mem_ref[...].astype(BF16)
    mk_t = jnp.dot(mem, wk_ref[...], preferred_element_type=F32).T
    mv = jnp.dot(mem, wv_ref[...], preferred_element_type=F32)
    feat_row = lax.broadcasted_iota(jnp.int32, mk_t.shape, 0) // MEM_HEAD_DIM
    feat_col = lax.broadcasted_iota(jnp.int32, mv.shape, 1) // MEM_HEAD_DIM
    for h in range(MEM_HEADS):
        seg = slice(h * MEM_TOKENS, (h + 1) * MEM_TOKENS)
        kbd_ref[:, seg] = jnp.where(feat_row == h, mk_t, 0.0).astype(BF16)
        vbd_ref[seg, :] = jnp.where(feat_col == h, mv, 0.0).astype(BF16)


def _mem_kv(mem, layer, wk, wv):
    nt = MEM_HEADS * MEM_TOKENS
    full2 = lambda shape: pl.BlockSpec(shape, lambda i: (0, 0))
    stack = lambda a: pl.BlockSpec((None,) + a.shape[1:], lambda i: (layer, 0, 0))
    return pl.pallas_call(
        _mem_kv_kernel,
        grid=(1,),
        in_specs=[full2(mem.shape), stack(wk), stack(wv)],
        out_specs=[full2((MEM_WIDTH, nt)), full2((nt, MEM_WIDTH))],
        out_shape=[jax.ShapeDtypeStruct((MEM_WIDTH, nt), BF16),
                   jax.ShapeDtypeStruct((nt, MEM_WIDTH), BF16)],
        compiler_params=_cparams("arbitrary"),
        name="mem_kv",
    )(mem, wk, wv)


def _mem_attn_kernel(q_ref, kbd_ref, vbd_ref, g_ref, o_ref):
    logits = jnp.dot(q_ref[...], kbd_ref[...], preferred_element_type=F32) * MEM_SCALE
    probs = []
    for h in range(MEM_HEADS):
        seg = logits[:, h * MEM_TOKENS:(h + 1) * MEM_TOKENS]
        e = jnp.exp(seg - jnp.max(seg, axis=-1, keepdims=True))
        probs.append((e / jnp.sum(e, axis=-1, keepdims=True)).astype(BF16))
    p = jnp.concatenate(probs, axis=-1)
    o = jnp.dot(p, vbd_ref[...], preferred_element_type=F32)
    o_ref[...] = (o * jax.nn.silu(g_ref[...])).astype(o_ref.dtype)


def _mem_attn(h2, kbd, vbd, h1):
    s = h2.shape[0]
    tm = min(s, 512)
    full2 = lambda shape: pl.BlockSpec(shape, lambda i: (0, 0))
    return pl.pallas_call(
        _mem_attn_kernel,
        grid=(s // tm,),
        in_specs=[pl.BlockSpec((tm, MEM_WIDTH), lambda i: (i, H2_MQ // MEM_WIDTH)),
                  full2(kbd.shape), full2(vbd.shape),
                  pl.BlockSpec((tm, MEM_WIDTH), lambda i: (i, H1_GM // MEM_WIDTH))],
        out_specs=pl.BlockSpec((tm, MEM_WIDTH), lambda i: (i, 0)),
        out_shape=jax.ShapeDtypeStruct((s, MEM_WIDTH), BF16),
        compiler_params=_cparams("parallel"),
        name="mem_attn",
    )(h2, kbd, vbd, h1)


def _out_ln_kernel(ya_ref, yb_ref, yc_ref, ym_ref, w_ref, x_ref, g_ref, b_ref, o_ref, obf_ref, *, alpha):
    tm = x_ref.shape[0]
    for rows in (slice(0, tm // 2), slice(tm // 2, tm)):
        y = None
        row = 0
        for y_ref in (ya_ref, yb_ref, yc_ref, ym_ref):
            width = y_ref.shape[1]
            part = jnp.dot(y_ref[rows, :], w_ref[row:row + width, :], preferred_element_type=F32)
            y = part if y is None else y + part
            row += width
        r = alpha * x_ref[rows, :] + y
        mu = jnp.mean(r, axis=-1, keepdims=True)
        rc = r - mu
        var = jnp.mean(rc * rc, axis=-1, keepdims=True)
        out = rc * lax.rsqrt(var + LN_EPS) * g_ref[...] + b_ref[...]
        o_ref[rows, :] = out
        obf_ref[rows, :] = out.astype(BF16)


def _out_ln(ya, yb, yc, ym, w, layer, x, g, b, alpha):
    s, d = x.shape
    tm = min(s, 512)
    row = lambda w: pl.BlockSpec((tm, w), lambda i: (i, 0))
    full = lambda a: pl.BlockSpec((None,) + a.shape[1:], lambda i: (layer, 0, 0))
    return pl.pallas_call(
        functools.partial(_out_ln_kernel, alpha=alpha),
        grid=(s // tm,),
        in_specs=[row(MLA_WIDTH), row(SG_WIDTH), row(SB_WIDTH), row(MEM_WIDTH),
                  full(w), row(d), full(g), full(b)],
        out_specs=[row(d), row(d)],
        out_shape=[jax.ShapeDtypeStruct((s, d), F32), jax.ShapeDtypeStruct((s, d), BF16)],
        compiler_params=_cparams("parallel"),
        name="out_ln",
    )(ya, yb, yc, ym, w, x, g, b)


def _prep_mla_weights(w_uq, w_ukv):
    depth = w_uq.shape[0]
    wq = w_uq.reshape(depth, MLA_Q_RANK, MLA_HEADS, MLA_NOPE + MLA_ROPE)
    wq = jnp.pad(wq, ((0, 0), (0, 0), (0, 0), (0, MLA_QK_PAD - MLA_NOPE - MLA_ROPE)))
    wqt = jnp.swapaxes(wq.reshape(depth, MLA_Q_RANK, MLA_HEADS * MLA_QK_PAD), 1, 2).astype(BF16)
    wkv = w_ukv.reshape(depth, MLA_KV_RANK, MLA_HEADS, MLA_NOPE + MLA_V)
    wk = wkv[..., :MLA_NOPE].reshape(depth, MLA_KV_RANK, MLA_WIDTH).astype(BF16)
    wvt = jnp.swapaxes(wkv[..., MLA_NOPE:].reshape(depth, MLA_KV_RANK, MLA_WIDTH), 1, 2).astype(BF16)
    return wqt, wk, wvt


def kernel(x, mem, positions, w_in, q_norm_g, w_uq, kv_norm_g, w_ukv, sg_ln_g, sg_ln_b, sg_w, sg_b,
           w_mem_k, w_mem_v, w_out, ln_g, ln_b):
    b, s, d = x.shape
    depth = w_in.shape[0]
    alpha = (2.0 * depth) ** 0.25
    sb_tc = min(s, ATT_TC)

    inv_freq = ROPE_THETA ** (-jnp.arange(0, MLA_ROPE, 2, dtype=F32) / MLA_ROPE)
    invf = jnp.concatenate([inv_freq, inv_freq, jnp.zeros((LANE - MLA_ROPE,), F32)])
    tri = jnp.concatenate(
        [(lax.broadcasted_iota(jnp.int32, (sb_tc, sb_tc), 1)
          > lax.broadcasted_iota(jnp.int32, (sb_tc, sb_tc), 0)).astype(BF16),
         jnp.ones((BF16_ROWS, sb_tc), BF16)], axis=0)
    w1, w2, w2q = _w_in_prep(jnp.swapaxes(w_in, 1, 2))
    wqt, wk, wvt = _prep_mla_weights(w_uq, w_ukv)
    wmk, wmv, wo = w_mem_k.astype(BF16), w_mem_v.astype(BF16), w_out.astype(BF16)
    row3 = lambda p: p[:, None, :]
    sg_b_t = jnp.swapaxes(sg_b, 1, 2)
    h2t_scale = jnp.concatenate([jnp.full((SB_WIDTH, 1), SB_SCALE * LOG2E, F32),
                                 jnp.ones((SB_WIDTH, 1), F32)], axis=0)

    outs = []
    for bi in range(b):
        xf = x[bi]
        xb = xf
        cos, sin, cos_t, sin_t = _rope_tables(positions[bi].reshape(s, 1), positions[bi].reshape(1, s),
                                              invf[None, :], invf[:, None])
        for l in range(depth):
            h1 = _matmul(xb, w1, l, F32, 1024, 1024, "in_proj_f32")
            h2 = _matmul(xb, w2, l, BF16, 1024, H2_W, "in_proj_bf16")
            h2t = _matmul_nt(w2q, l, xb, h2t_scale, BF16, 1024, H2T_W, "in_proj_bf16_t")
            qt, k, vt = _mla_prep(h1, l, row3(q_norm_g), row3(kv_norm_g), wqt, wk, wvt, cos, sin, cos_t, sin_t)
            ya = _mla_attn(qt, k, vt, h1)
            yc = _sb_attn(h2, h2t, h1, tri)
            yb = _gmlp(h1, l, row3(sg_ln_g), row3(sg_ln_b), sg_w, sg_b_t)
            kbd, vbd = _mem_kv(mem[bi], l, wmk, wmv)
            ym = _mem_attn(h2, kbd, vbd, h1)
            xf, xb = _out_ln(ya, yb, yc, ym, wo, l, xf, row3(ln_g), row3(ln_b), alpha)
        outs.append(xf)
    return outs[0][None] if b == 1 else jnp.stack(outs, axis=0)
```

```python
import functools
import math

import jax
import jax.numpy as jnp
from jax import lax
from jax.experimental import pallas as pl
from jax.experimental.pallas import tpu as pltpu

F32 = jnp.float32
BF16 = jnp.bfloat16

D_MODEL = 2048
CHUNK = 64
MLA_HEADS = 6
MLA_NOPE = 128
MLA_ROPE = 64
MLA_V = 128
MLA_Q_RANK = 512
MLA_KV_RANK = 256
MLA_WIDTH = MLA_HEADS * MLA_V
ROPE_THETA = 10000.0
SG_GROUPS = 4
SG_GROUP_CH = 128
SG_WIDTH = SG_GROUPS * SG_GROUP_CH
SG_CHUNK = 128
SB_HEADS = 4
SB_HEAD_DIM = 128
SB_WIDTH = SB_HEADS * SB_HEAD_DIM
MEM_TOKENS = 256
MEM_HEADS = 4
MEM_HEAD_DIM = 64
MEM_WIDTH = MEM_HEADS * MEM_HEAD_DIM
LN_EPS = 1e-5
RMS_EPS = 1e-6

LOG2E = math.log2(math.e)
MLA_SCALE = 1.0 / math.sqrt(MLA_NOPE + MLA_ROPE)
SB_SCALE = 1.0 / math.sqrt(SB_HEAD_DIM)
MEM_SCALE = 1.0 / math.sqrt(MEM_HEAD_DIM)

LANE = 128
BF16_ROWS = 16
MXU_TILE = 256
MLA_QK_PAD = 256
MLA_V_AUG = MLA_V + BF16_ROWS
NEG_BIG = -1e30
VMEM_LIMIT = 48 * 1024 * 1024
MLA_TQ = 1024
SB_TQ = 512
ATT_TC = MXU_TILE

H1_LAT_W = 1024
H1_SGU = 1024
H1_SGV = 1536
H1_GB = 2048
H1_GA = 2560
H1_GC = 3328
H1_GM = 3840
H1_W = 4096
H2_SBK = 0
H2_MQ = 512
H2_W = 768
H2T_SBQ = 0
H2T_SBV = 512
H2T_W = 1024

_NT = (((1,), (1,)), ((), ()))


def _cparams(*sem, flags=None):
    return pltpu.CompilerParams(dimension_semantics=sem, vmem_limit_bytes=VMEM_LIMIT, flags=flags)


ATT_FLAGS = None


def _rope_table_kernel(pos_col_ref, pos_row_ref, invf_row_ref, invf_col_ref, cos_ref, sin_ref, cos_t_ref, sin_t_ref):
    ang = pos_col_ref[...].astype(F32) * invf_row_ref[...]
    cos_ref[...] = jnp.cos(ang)
    sin_ref[...] = jnp.sin(ang)
    ang_t = invf_col_ref[...] * pos_row_ref[...].astype(F32)
    cos_t_ref[...] = jnp.cos(ang_t)
    sin_t_ref[...] = jnp.sin(ang_t)


def _rope_tables(pos_col, pos_row, invf_row, invf_col):
    s = pos_col.shape[0]
    tm = min(s, 1024)
    return pl.pallas_call(
        _rope_table_kernel,
        grid=(s // tm,),
        in_specs=[pl.BlockSpec((tm, 1), lambda i: (i, 0)),
                  pl.BlockSpec((1, tm), lambda i: (0, i)),
                  pl.BlockSpec((1, LANE), lambda i: (0, 0)),
                  pl.BlockSpec((LANE, 1), lambda i: (0, 0))],
        out_specs=[pl.BlockSpec((tm, LANE), lambda i: (i, 0)),
                   pl.BlockSpec((tm, LANE), lambda i: (i, 0)),
                   pl.BlockSpec((LANE, tm), lambda i: (0, i)),
                   pl.BlockSpec((LANE, tm), lambda i: (0, i))],
        out_shape=[jax.ShapeDtypeStruct((s, LANE), F32)] * 2 + [jax.ShapeDtypeStruct((LANE, s), F32)] * 2,
        compiler_params=_cparams("parallel"),
        name="rope_tables",
    )(pos_col, pos_row, invf_row, invf_col)


_IN_SEGMENTS = ("c_q", "c_kv", "k_pe", "g_a", "sg_u", "sg_v", "g_b", "sb_q", "sb_k", "sb_v", "g_c", "m_q", "g_m")
_IN_WIDTHS = (MLA_Q_RANK, MLA_KV_RANK, MLA_ROPE, MLA_WIDTH, SG_WIDTH, SG_WIDTH, SG_WIDTH,
              SB_WIDTH, SB_WIDTH, SB_WIDTH, SB_WIDTH, MEM_WIDTH, MEM_WIDTH)
_IN_OFFSETS = tuple(sum(_IN_WIDTHS[:i]) for i in range(len(_IN_WIDTHS) + 1))
IN_ROWS = {name: (_IN_OFFSETS[i], _IN_OFFSETS[i + 1]) for i, name in enumerate(_IN_SEGMENTS)}


def _w_in_prep_kernel(wt_ref, w1_ref, w2_ref, w2q_ref):
    def put(dst, off, name):
        a, b = IN_ROWS[name]
        dst[off:off + b - a, :] = wt_ref[a:b, :].astype(BF16)

    lat_end = MLA_Q_RANK + MLA_KV_RANK + MLA_ROPE
    for name, off in (("c_q", 0), ("c_kv", MLA_Q_RANK), ("k_pe", MLA_Q_RANK + MLA_KV_RANK),
                      ("sg_u", H1_SGU), ("sg_v", H1_SGV), ("g_b", H1_GB), ("g_a", H1_GA), ("g_c", H1_GC),
                      ("g_m", H1_GM)):
        put(w1_ref, off, name)
    w1_ref[lat_end:H1_LAT_W, :] = jnp.zeros((H1_LAT_W - lat_end, w1_ref.shape[1]), BF16)
    put(w2_ref, H2_SBK, "sb_k")
    put(w2_ref, H2_MQ, "m_q")
    put(w2q_ref, H2T_SBQ, "sb_q")
    put(w2q_ref, H2T_SBV, "sb_v")


def _w_in_prep(w_in_t):
    depth, n, k = w_in_t.shape
    tk = 256
    blk = lambda rows: pl.BlockSpec((None, rows, tk), lambda l, i: (l, 0, i))
    return pl.pallas_call(
        _w_in_prep_kernel,
        grid=(depth, k // tk),
        in_specs=[blk(n)],
        out_specs=[blk(H1_W), blk(H2_W), blk(H2T_W)],
        out_shape=[jax.ShapeDtypeStruct((depth, H1_W, k), BF16),
                   jax.ShapeDtypeStruct((depth, H2_W, k), BF16),
                   jax.ShapeDtypeStruct((depth, H2T_W, k), BF16)],
        compiler_params=_cparams("parallel", "parallel"),
        name="w_in_prep",
    )(w_in_t)


def _mm_kernel(x_ref, wt_ref, o_ref):
    x = x_ref[...].astype(BF16)
    o_ref[...] = lax.dot_general(x, wt_ref[...], _NT, preferred_element_type=F32).astype(o_ref.dtype)


def _matmul(x, wt, layer, out_dtype, tm, tn, name):
    m, k = x.shape
    n = wt.shape[1]
    tm = min(tm, m)
    return pl.pallas_call(
        _mm_kernel,
        grid=(m // tm, n // tn),
        in_specs=[pl.BlockSpec((tm, k), lambda i, j: (i, 0)),
                  pl.BlockSpec((None, tn, k), lambda i, j: (layer, j, 0))],
        out_specs=pl.BlockSpec((tm, tn), lambda i, j: (i, j)),
        out_shape=jax.ShapeDtypeStruct((m, n), out_dtype),
        compiler_params=_cparams("parallel", "parallel"),
        name=name,
    )(x, wt)


def _mm_nt_kernel(wt_ref, x_ref, scale_ref, o_ref):
    x = x_ref[...].astype(BF16)
    acc = lax.dot_general(wt_ref[...], x, _NT, preferred_element_type=F32)
    o_ref[...] = (acc * scale_ref[...]).astype(o_ref.dtype)


def _matmul_nt(wt, layer, x, row_scale, out_dtype, tm, tn, name):
    n, k = wt.shape[1:]
    m = x.shape[0]
    tm = min(tm, m)
    return pl.pallas_call(
        _mm_nt_kernel,
        grid=(m // tm, n // tn),
        in_specs=[pl.BlockSpec((None, tn, k), lambda i, j: (layer, j, 0)),
                  pl.BlockSpec((tm, k), lambda i, j: (i, 0)),
                  pl.BlockSpec((tn, 1), lambda i, j: (j, 0))],
        out_specs=pl.BlockSpec((tn, tm), lambda i, j: (j, i)),
        out_shape=jax.ShapeDtypeStruct((n, m), out_dtype),
        compiler_params=_cparams("parallel", "parallel"),
        name=name,
    )(wt, x, row_scale)


def _rms(x, g):
    ms = jnp.mean(x * x, axis=-1, keepdims=True)
    return x * lax.rsqrt(ms + RMS_EPS) * g


def _mla_prep_kernel(lat_ref, qg_ref, kvg_ref, wqt_ref, wk_ref, wvt_ref, cos_ref, sin_ref, cos_t_ref, sin_t_ref,
                     qt_ref, k_ref, vt_ref):
    half = MLA_ROPE // 2
    cqn = _rms(lat_ref[:, 0:MLA_Q_RANK], qg_ref[...]).astype(BF16)
    ckvn = _rms(lat_ref[:, MLA_Q_RANK:MLA_Q_RANK + MLA_KV_RANK], kvg_ref[...]).astype(BF16)
    qa_t = lax.dot_general(wqt_ref[...], cqn, _NT, preferred_element_type=F32) * (MLA_SCALE * LOG2E)
    kn = jnp.dot(ckvn, wk_ref[...], preferred_element_type=F32)
    v_t = lax.dot_general(wvt_ref[...], ckvn, _NT, preferred_element_type=F32).astype(BF16)
    ones = jnp.ones((BF16_ROWS, v_t.shape[1]), BF16)
    cos_t = cos_t_ref[...]
    sin_t = sin_t_ref[...]
    kpe = lat_ref[:, MLA_Q_RANK + MLA_KV_RANK:MLA_Q_RANK + MLA_KV_RANK + LANE]
    lane = lax.broadcasted_iota(jnp.int32, kpe.shape, 1)
    kpe_rot = jnp.where(lane < half, -pltpu.roll(kpe, LANE - half, 1),
                        jnp.where(lane < MLA_ROPE, pltpu.roll(kpe, half, 1), 0.0))
    krot = (kpe * cos_ref[...] + kpe_rot * sin_ref[...]).astype(BF16)
    for h in range(MLA_HEADS):
        c0 = h * MLA_QK_PAD
        qt_ref[c0:c0 + LANE, :] = qa_t[c0:c0 + LANE, :].astype(BF16)
        qr = qa_t[c0 + LANE:c0 + 2 * LANE, :]
        qr_rot = jnp.concatenate([-qr[half:MLA_ROPE], qr[0:half], qr[MLA_ROPE:LANE]], axis=0)
        qt_ref[c0 + LANE:c0 + 2 * LANE, :] = (qr * cos_t + qr_rot * sin_t).astype(BF16)
        k_ref[:, c0:c0 + LANE] = kn[:, h * LANE:(h + 1) * LANE].astype(BF16)
        k_ref[:, c0 + LANE:c0 + 2 * LANE] = krot
        vt_ref[h, 0:MLA_V, :] = v_t[h * MLA_V:(h + 1) * MLA_V, :]
        vt_ref[h, MLA_V:MLA_V_AUG, :] = ones


def _mla_prep(h1, layer, qg, kvg, wqt, wk, wvt, cos, sin, cos_t, sin_t):
    s = h1.shape[0]
    tm = min(s, 512)
    nq = MLA_HEADS * MLA_QK_PAD
    full = lambda a: pl.BlockSpec((None,) + a.shape[1:], lambda i: (layer, 0, 0))
    rows = lambda w: pl.BlockSpec((tm, w), lambda i: (i, 0))
    cols = lambda w: pl.BlockSpec((w, tm), lambda i: (0, i))
    return pl.pallas_call(
        _mla_prep_kernel,
        grid=(s // tm,),
        in_specs=[rows(H1_LAT_W), full(qg), full(kvg), full(wqt), full(wk), full(wvt),
                  rows(LANE), rows(LANE), cols(LANE), cols(LANE)],
        out_specs=[cols(nq), rows(nq), pl.BlockSpec((MLA_HEADS, MLA_V_AUG, tm), lambda i: (0, 0, i))],
        out_shape=[jax.ShapeDtypeStruct((nq, s), BF16),
                   jax.ShapeDtypeStruct((s, nq), BF16),
                   jax.ShapeDtypeStruct((MLA_HEADS, MLA_V_AUG, s), BF16)],
        compiler_params=_cparams("parallel"),
        name="mla_prep",
    )(h1, qg, kvg, wqt, wk, wvt, cos, sin, cos_t, sin_t)


def _run_pipeline(n_items, stage_a, stage_b, stage_c_weights, stage_c_values):
    if n_items == 0:
        return

    def stage_c(t, slot):
        stage_c_weights(t, slot)
        stage_c_values(t, slot)

    def trip(i, slot):
        stage_c_weights(i - 2, slot)
        stage_a(i, slot)
        stage_c_values(i - 2, slot)
        stage_b(i - 1, 1 - slot)

    stage_a(0, 0)
    if n_items >= 2:
        stage_a(1, 1)
    stage_b(0, 0)
    steady = max(n_items - 2, 0)

    def body(ii, carry):
        i = 2 + 2 * ii
        trip(i, 0)
        trip(i + 1, 1)
        return carry

    lax.fori_loop(0, steady // 2, body, 0)
    if steady % 2:
        trip(n_items - 1, (n_items - 1) % 2)
    if n_items >= 2:
        stage_c_weights(n_items - 2, n_items % 2)
        stage_b(n_items - 1, (n_items - 1) % 2)
        stage_c_values(n_items - 2, n_items % 2)
    stage_c(n_items - 1, (n_items - 1) % 2)


def _mla_attn_kernel(qt_ref, k_ref, vt_ref, g_ref, o_ref, m_sc, acc_sc, s_buf, p_buf, al_buf, *, tq, tc):
    n = pl.program_id(1) + 1
    m_sc[...] = jnp.full(m_sc.shape, NEG_BIG, F32)
    acc_sc[...] = jnp.zeros(acc_sc.shape, F32)
    col_tiles = [slice(c * tc, (c + 1) * tc) for c in range(tq // tc)]

    def key_rows(u, n_keys):
        j = jnp.where(u == 0, n - 1, u - 1)
        return pl.ds(pl.multiple_of(j * tq, tq), n_keys)

    def stage_scores(u, slot):
        k = k_ref[key_rows(u, tq), :]
        for cols in col_tiles:
            s_buf[slot, :, cols] = jnp.dot(k, qt_ref[:, cols], preferred_element_type=F32)

    def stage_scores_diagonal(slot):
        for c, cols in enumerate(col_tiles):
            n_keys = (c + 1) * tc
            k = k_ref[key_rows(0, n_keys), :]
            s_buf[slot, 0:n_keys, cols] = jnp.dot(k, qt_ref[:, cols], preferred_element_type=F32)

    def stage_softmax(slot, masked):
        for c, cols in enumerate(col_tiles):
            n_keys = (c + 1) * tc if masked else tq
            s = s_buf[slot, 0:n_keys, cols]
            if masked:
                kc = lax.broadcasted_iota(jnp.int32, s.shape, 0) // CHUNK
                qc = (lax.broadcasted_iota(jnp.int32, s.shape, 1) + c * tc) // CHUNK
                s = jnp.where(kc <= qc, s, NEG_BIG)
            m_prev = m_sc[:, cols]
            m_new = jnp.maximum(m_prev, jnp.max(s, axis=0, keepdims=True))
            p_buf[slot, 0:n_keys, cols] = jnp.exp2(s - m_new).astype(BF16)
            if n_keys < tq:
                p_buf[slot, n_keys:tq, cols] = jnp.zeros((tq - n_keys, tc), BF16)
            al_buf[slot, :, cols] = jnp.exp2(m_prev - m_new)
            m_sc[:, cols] = m_new

    def stage_values(u, slot):
        vt = vt_ref[:, key_rows(u, tq)]
        for cols in col_tiles:
            pv = jnp.dot(vt, p_buf[slot, :, cols], preferred_element_type=F32)
            acc_sc[:, cols] = al_buf[slot, :, cols] * acc_sc[:, cols] + pv

    def trip(i, slot):
        stage_scores(i, slot)
        stage_values(i - 2, slot)
        stage_softmax(1 - slot, False)

    def drain(last_slot):
        stage_values(n - 2, 1 - last_slot)
        stage_softmax(last_slot, False)
        stage_values(n - 1, last_slot)

    stage_scores_diagonal(0)

    @pl.when(n >= 2)
    def _():
        stage_softmax(0, True)
        stage_scores(1, 1)

    def body(ii, carry):
        i = 2 + 2 * ii
        trip(i, 0)
        trip(i + 1, 1)
        return carry

    lax.fori_loop(0, (n - 2) // 2, body, 0)
    n_odd = n % 2 == 1

    @pl.when(jnp.logical_and(n_odd, n >= 3))
    def _():
        trip(n - 1, 0)
        drain(0)

    @pl.when(jnp.logical_not(n_odd))
    def _():
        drain(1)

    @pl.when(n == 1)
    def _():
        stage_softmax(0, True)
        stage_values(0, 0)

    o = (acc_sc[0:MLA_V, :] / acc_sc[MLA_V:MLA_V + 1, :]).T
    o_ref[...] = (o * jax.nn.silu(g_ref[...])).astype(o_ref.dtype)


def _pair_table(nq):
    pairs = [(qi, qi - d) for d in range(1, nq) for qi in range(d, nq)] or [(0, 0)]
    return jnp.asarray(pairs, jnp.int32).T


def _mla_attn(qt, k, vt, h1):
    s = k.shape[0]
    tq = min(MLA_TQ, s)
    tc = min(ATT_TC, tq)
    ga_blk = H1_GA // LANE
    return pl.pallas_call(
        functools.partial(_mla_attn_kernel, tq=tq, tc=tc),
        grid=(MLA_HEADS, s // tq),
        in_specs=[pl.BlockSpec((MLA_QK_PAD, tq), lambda h, i: (h, i)),
                  pl.BlockSpec((s, MLA_QK_PAD), lambda h, i: (0, h)),
                  pl.BlockSpec((None, MLA_V_AUG, s), lambda h, i: (h, 0, 0)),
                  pl.BlockSpec((tq, LANE), lambda h, i: (i, ga_blk + h))],
        out_specs=pl.BlockSpec((tq, MLA_V), lambda h, i: (i, h)),
        out_shape=jax.ShapeDtypeStruct((s, MLA_WIDTH), BF16),
        scratch_shapes=[pltpu.VMEM((1, tq), F32), pltpu.VMEM((MLA_V_AUG, tq), F32),
                        pltpu.VMEM((2, tq, tq), F32), pltpu.VMEM((2, tq, tq), BF16), pltpu.VMEM((2, 1, tq), F32)],
        compiler_params=_cparams("parallel", "parallel", flags=ATT_FLAGS),
        name="mla_attn",
    )(qt, k, vt, h1)


def _sb_attn_kernel(tab_ref, qt_ref, k_ref, vt_ref, tri_ref, g_ref, o_ref, r_st, acc_st,
                    lb_buf, hi_buf, x_buf, cs_buf, a_buf, *, tq, tc, nq):
    n_sub = tq // tc
    n_off = nq * (nq - 1) // 2
    r_st[...] = jnp.zeros(r_st.shape, F32)
    acc_st[...] = jnp.zeros(acc_st.shape, F32)
    sub = [slice(t * tc, (t + 1) * tc) for t in range(n_sub)]
    strict = (lax.broadcasted_iota(jnp.int32, (tc, tc), 0)
              < lax.broadcasted_iota(jnp.int32, (tc, tc), 1))

    def q_cols(qi, c):
        return pl.ds(pl.multiple_of(qi * tq + c * tc, tc), tc)

    def stage_a(qi, j, slot, diag):
        for kk in range(n_sub):
            k = k_ref[pl.ds(pl.multiple_of(j * tq + kk * tc, tc), tc), :]
            for c in range(n_sub):
                if diag and kk > c:
                    hi_buf[slot, sub[kk], sub[c]] = jnp.zeros((tc, tc), BF16)
                    continue
                y = jnp.dot(k, qt_ref[:, q_cols(qi, c)], preferred_element_type=F32)
                neg_part = jnp.minimum(y, 0.0)
                neg_relu = neg_part - y
                t = jnp.log(1.0 + jnp.exp2(neg_part + neg_relu)) * LOG2E
                log_1mb = neg_relu - t
                if diag and kk == c:
                    log_1mb = jnp.where(strict, log_1mb, 0.0)
                lb_buf[slot, sub[kk], sub[c]] = neg_part - t
                hi_buf[slot, sub[kk], sub[c]] = log_1mb.astype(BF16)

    def stage_b(slot):
        tri = tri_ref[...]
        for kk in range(n_sub):
            for c in range(n_sub):
                r = jnp.dot(tri, hi_buf[slot, sub[kk], sub[c]], preferred_element_type=F32)
                x_buf[slot, sub[kk], sub[c]] = r[:tc] + lb_buf[slot, sub[kk], sub[c]]
                cs_buf[slot, kk, :, sub[c]] = r[tc:tc + 1]

    def stage_c_weights(qi, slot, diag):
        for c in range(n_sub):
            run = r_st[:, q_cols(qi, c)]
            for kk in range(n_sub - 1, -1, -1):
                if diag and kk > c:
                    a_buf[sub[kk], sub[c]] = jnp.zeros((tc, tc), BF16)
                    continue
                a = jnp.exp2(x_buf[slot, sub[kk], sub[c]] + run)
                if diag and kk == c:
                    a = jnp.where(strict, a, 0.0)
                a_buf[sub[kk], sub[c]] = a.astype(BF16)
                run = run + cs_buf[slot, kk, :, sub[c]]
            r_st[:, q_cols(qi, c)] = run

    def stage_c_values(qi, j):
        vt = vt_ref[:, pl.ds(pl.multiple_of(j * tq, tq), tq)]
        for c in range(n_sub):
            acc_st[:, q_cols(qi, c)] += jnp.dot(vt, a_buf[:, sub[c]], preferred_element_type=F32)

    _run_pipeline(nq,
                  lambda t, slot: stage_a(t, t, slot, True),
                  lambda t, slot: stage_b(slot),
                  lambda t, slot: stage_c_weights(t, slot, True),
                  lambda t, slot: stage_c_values(t, t))
    _run_pipeline(n_off,
                  lambda t, slot: stage_a(tab_ref[0, t], tab_ref[1, t], slot, False),
                  lambda t, slot: stage_b(slot),
                  lambda t, slot: stage_c_weights(tab_ref[0, t], slot, False),
                  lambda t, slot: stage_c_values(tab_ref[0, t], tab_ref[1, t]))

    def finish(qi, carry):
        rows = pl.ds(pl.multiple_of(qi * tq, tq), tq)
        o_ref[rows, :] = (acc_st[:, rows].T * jax.nn.silu(g_ref[rows, :])).astype(o_ref.dtype)
        return carry

    lax.fori_loop(0, nq, finish, 0)


def _sb_attn(h2, h2t, h1, tri):
    s = h2.shape[0]
    tq = min(SB_TQ, s)
    tc = tri.shape[1]
    nq = s // tq
    qb, kb, vb, gb = H2T_SBQ // LANE, H2_SBK // LANE, H2T_SBV // LANE, H1_GC // LANE
    return pl.pallas_call(
        functools.partial(_sb_attn_kernel, tq=tq, tc=tc, nq=nq),
        grid=(SB_HEADS,),
        in_specs=[pl.BlockSpec(memory_space=pltpu.SMEM),
                  pl.BlockSpec((LANE, s), lambda h: (qb + h, 0)),
                  pl.BlockSpec((s, LANE), lambda h: (0, kb + h)),
                  pl.BlockSpec((LANE, s), lambda h: (vb + h, 0)),
                  pl.BlockSpec(tri.shape, lambda h: (0, 0)),
                  pl.BlockSpec((s, LANE), lambda h: (0, gb + h))],
        out_specs=pl.BlockSpec((s, LANE), lambda h: (0, h)),
        out_shape=jax.ShapeDtypeStruct((s, SB_WIDTH), BF16),
        scratch_shapes=[pltpu.VMEM((1, s), F32), pltpu.VMEM((SB_HEAD_DIM, s), F32),
                        pltpu.VMEM((2, tq, tq), F32), pltpu.VMEM((2, tq, tq), BF16),
                        pltpu.VMEM((2, tq, tq), F32), pltpu.VMEM((2, tq // tc, 1, tq), F32),
                        pltpu.VMEM((tq, tq), BF16)],
        compiler_params=_cparams("parallel", flags=ATT_FLAGS),
        name="sb_attn",
    )(_pair_table(nq), h2t, h2, h2t, tri, h1)


def _gmlp_kernel(u_ref, v_ref, g_ref, lng_ref, lnb_ref, w_ref, b_ref, o_ref, *, n_chunks):
    gv = jax.nn.gelu(v_ref[...])
    mu = jnp.mean(gv, axis=-1, keepdims=True)
    vc = gv - mu
    var = jnp.mean(vc * vc, axis=-1, keepdims=True)
    vn = (vc * lax.rsqrt(var + LN_EPS) * lng_ref[...] + lnb_ref[...]).astype(BF16)
    t_chunk = lax.broadcasted_iota(jnp.int32, (SG_CHUNK, SG_CHUNK), 0) // CHUNK
    s_chunk = lax.broadcasted_iota(jnp.int32, (SG_CHUNK, SG_CHUNK), 1) // CHUNK
    mask = (s_chunk <= t_chunk).astype(F32)
    for g in range(SG_GROUPS):
        w_sp = (w_ref[g] * mask).astype(BF16)
        bias = b_ref[:, g:g + 1]
        cols = slice(g * SG_GROUP_CH, (g + 1) * SG_GROUP_CH)
        for n in range(n_chunks):
            rows = slice(n * SG_CHUNK, (n + 1) * SG_CHUNK)
            mixed = jnp.dot(w_sp, vn[rows, cols], preferred_element_type=F32) + bias
            o_b = jax.nn.gelu(u_ref[rows, cols]) * mixed
            o_ref[rows, cols] = (o_b * jax.nn.silu(g_ref[rows, cols])).astype(o_ref.dtype)


def _mem_kv_kernel(mem_ref, wk_ref, wv_ref, kbd_ref, vbd_ref):
    mem = mem_ref[...].astype(BF16)
    mk_t = jnp.dot(mem, wk_ref[...], preferred_element_type=F32).T
    mv = jnp.dot(mem, wv_ref[...], preferred_element_type=F32)
    feat_row = lax.broadcasted_iota(jnp.int32, mk_t.shape, 0) // MEM_HEAD_DIM
    feat_col = lax.broadcasted_iota(jnp.int32, mv.shape, 1) // MEM_HEAD_DIM
    for h in range(MEM_HEADS):
        seg = slice(h * MEM_TOKENS, (h + 1) * MEM_TOKENS)
        kbd_ref[:, seg] = jnp.where(feat_row == h, mk_t, 0.0).astype(BF16)
        vbd_ref[seg, :] = jnp.where(feat_col == h, mv, 0.0).astype(BF16)


def _mem_kv(mem, layer, wk, wv):
    nt = MEM_HEADS * MEM_TOKENS
    full2 = lambda shape: pl.BlockSpec(shape, lambda i: (0, 0))
    stack = lambda a: pl.BlockSpec((None,) + a.shape[1:], lambda i: (layer, 0, 0))
    return pl.pallas_call(
        _mem_kv_kernel,
        grid=(1,),
        in_specs=[full2(mem.shape), stack(wk), stack(wv)],
        out_specs=[full2((MEM_WIDTH, nt)), full2((nt, MEM_WIDTH))],
        out_shape=[jax.ShapeDtypeStruct((MEM_WIDTH, nt), BF16),
                   jax.ShapeDtypeStruct((nt, MEM_WIDTH), BF16)],
        compiler_params=_cparams("arbitrary"),
        name="mem_kv",
    )(mem, wk, wv)


def _mem_attn_kernel(q_ref, kbd_ref, vbd_ref, g_ref, o_ref):
    logits = jnp.dot(q_ref[...], kbd_ref[...], preferred_element_type=F32) * MEM_SCALE
    probs = []
    for h in range(MEM_HEADS):
        seg = logits[:, h * MEM_TOKENS:(h + 1) * MEM_TOKENS]
        e = jnp.exp(seg - jnp.max(seg, axis=-1, keepdims=True))
        probs.append((e / jnp.sum(e, axis=-1, keepdims=True)).astype(BF16))
    p = jnp.concatenate(probs, axis=-1)
    o = jnp.dot(p, vbd_ref[...], preferred_element_type=F32)
    o_ref[...] = (o * jax.nn.silu(g_ref[...])).astype(o_ref.dtype)


def _gmlp_mem_kernel(u_ref, v_ref, gb_ref, lng_ref, lnb_ref, w_ref, b_ref, q_ref, kbd_ref, vbd_ref, gm_ref,
                     yb_ref, ym_ref, *, n_chunks):
    _mem_attn_kernel(q_ref, kbd_ref, vbd_ref, gm_ref, ym_ref)
    _gmlp_kernel(u_ref, v_ref, gb_ref, lng_ref, lnb_ref, w_ref, b_ref, yb_ref, n_chunks=n_chunks)


def _gmlp_mem(h1, h2, layer, lng, lnb, w, b_t, kbd, vbd):
    s = h1.shape[0]
    tm = min(s, 512)
    full = lambda a: pl.BlockSpec((None,) + a.shape[1:], lambda i: (layer,) + (0,) * (a.ndim - 1))
    full2 = lambda shape: pl.BlockSpec(shape, lambda i: (0, 0))
    return pl.pallas_call(
        functools.partial(_gmlp_mem_kernel, n_chunks=tm // SG_CHUNK),
        grid=(s // tm,),
        in_specs=[pl.BlockSpec((tm, SG_WIDTH), lambda i: (i, H1_SGU // SG_WIDTH)),
                  pl.BlockSpec((tm, SG_WIDTH), lambda i: (i, H1_SGV // SG_WIDTH)),
                  pl.BlockSpec((tm, SG_WIDTH), lambda i: (i, H1_GB // SG_WIDTH)),
                  full(lng), full(lnb), full(w), full(b_t),
                  pl.BlockSpec((tm, MEM_WIDTH), lambda i: (i, H2_MQ // MEM_WIDTH)),
                  full2(kbd.shape), full2(vbd.shape),
                  pl.BlockSpec((tm, MEM_WIDTH), lambda i: (i, H1_GM // MEM_WIDTH))],
        out_specs=[pl.BlockSpec((tm, SG_WIDTH), lambda i: (i, 0)),
                   pl.BlockSpec((tm, MEM_WIDTH), lambda i: (i, 0))],
        out_shape=[jax.ShapeDtypeStruct((s, SG_WIDTH), BF16),
                   jax.ShapeDtypeStruct((s, MEM_WIDTH), BF16)],
        compiler_params=_cparams("parallel"),
        name="gmlp_mem",
    )(h1, h1, h1, lng, lnb, w, b_t, h2, kbd, vbd, h1)


def _out_ln_kernel(ya_ref, yb_ref, yc_ref, ym_ref, w_ref, x_ref, g_ref, b_ref, o_ref, obf_ref, *, alpha):
    tm = x_ref.shape[0]
    for rows in (slice(0, tm // 2), slice(tm // 2, tm)):
        y = None
        row = 0
        for y_ref in (ya_ref, yb_ref, yc_ref, ym_ref):
            width = y_ref.shape[1]
            part = jnp.dot(y_ref[rows, :], w_ref[row:row + width, :], preferred_element_type=F32)
            y = part if y is None else y + part
            row += width
        r = alpha * x_ref[rows, :] + y
        mu = jnp.mean(r, axis=-1, keepdims=True)
        rc = r - mu
        var = jnp.mean(rc * rc, axis=-1, keepdims=True)
        out = rc * lax.rsqrt(var + LN_EPS) * g_ref[...] + b_ref[...]
        o_ref[rows, :] = out
        obf_ref[rows, :] = out.astype(BF16)


def _out_ln(ya, yb, yc, ym, w, layer, x, g, b, alpha):
    s, d = x.shape
    tm = min(s, 512)
    row = lambda w: pl.BlockSpec((tm, w), lambda i: (i, 0))
    full = lambda a: pl.BlockSpec((None,) + a.shape[1:], lambda i: (layer, 0, 0))
    return pl.pallas_call(
        functools.partial(_out_ln_kernel, alpha=alpha),
        grid=(s // tm,),
        in_specs=[row(MLA_WIDTH), row(SG_WIDTH), row(SB_WIDTH), row(MEM_WIDTH),
                  full(w), row(d), full(g), full(b)],
        out_specs=[row(d), row(d)],
        out_shape=[jax.ShapeDtypeStruct((s, d), F32), jax.ShapeDtypeStruct((s, d), BF16)],
        compiler_params=_cparams("parallel"),
        name="out_ln",
    )(ya, yb, yc, ym, w, x, g, b)


def _prep_mla_weights(w_uq, w_ukv):
    depth = w_uq.shape[0]
    wq = w_uq.reshape(depth, MLA_Q_RANK, MLA_HEADS, MLA_NOPE + MLA_ROPE)
    wq = jnp.pad(wq, ((0, 0), (0, 0), (0, 0), (0, MLA_QK_PAD - MLA_NOPE - MLA_ROPE)))
    wqt = jnp.swapaxes(wq.reshape(depth, MLA_Q_RANK, MLA_HEADS * MLA_QK_PAD), 1, 2).astype(BF16)
    wkv = w_ukv.reshape(depth, MLA_KV_RANK, MLA_HEADS, MLA_NOPE + MLA_V)
    wk = wkv[..., :MLA_NOPE].reshape(depth, MLA_KV_RANK, MLA_WIDTH).astype(BF16)
    wvt = jnp.swapaxes(wkv[..., MLA_NOPE:].reshape(depth, MLA_KV_RANK, MLA_WIDTH), 1, 2).astype(BF16)
    return wqt, wk, wvt


def kernel(x, mem, positions, w_in, q_norm_g, w_uq, kv_norm_g, w_ukv, sg_ln_g, sg_ln_b, sg_w, sg_b,
           w_mem_k, w_mem_v, w_out, ln_g, ln_b):
    b, s, d = x.shape
    depth = w_in.shape[0]
    alpha = (2.0 * depth) ** 0.25
    sb_tc = min(s, ATT_TC)

    inv_freq = ROPE_THETA ** (-jnp.arange(0, MLA_ROPE, 2, dtype=F32) / MLA_ROPE)
    invf = jnp.concatenate([inv_freq, inv_freq, jnp.zeros((LANE - MLA_ROPE,), F32)])
    tri = jnp.concatenate(
        [(lax.broadcasted_iota(jnp.int32, (sb_tc, sb_tc), 1)
          > lax.broadcasted_iota(jnp.int32, (sb_tc, sb_tc), 0)).astype(BF16),
         jnp.ones((BF16_ROWS, sb_tc), BF16)], axis=0)
    w1, w2, w2q = _w_in_prep(jnp.swapaxes(w_in, 1, 2))
    wqt, wk, wvt = _prep_mla_weights(w_uq, w_ukv)
    wmk, wmv, wo = w_mem_k.astype(BF16), w_mem_v.astype(BF16), w_out.astype(BF16)
    row3 = lambda p: p[:, None, :]
    sg_b_t = jnp.swapaxes(sg_b, 1, 2)
    h2t_scale = jnp.concatenate([jnp.full((SB_WIDTH, 1), SB_SCALE * LOG2E, F32),
                                 jnp.ones((SB_WIDTH, 1), F32)], axis=0)

    outs = []
    for bi in range(b):
        xf = x[bi]
        xb = xf
        cos, sin, cos_t, sin_t = _rope_tables(positions[bi].reshape(s, 1), positions[bi].reshape(1, s),
                                              invf[None, :], invf[:, None])
        for l in range(depth):
            h1 = _matmul(xb, w1, l, F32, 1024, 1024, "in_proj_f32")
            h2 = _matmul(xb, w2, l, BF16, 1024, H2_W, "in_proj_bf16")
            h2t = _matmul_nt(w2q, l, xb, h2t_scale, BF16, 1024, H2T_W, "in_proj_bf16_t")
            qt, k, vt = _mla_prep(h1, l, row3(q_norm_g), row3(kv_norm_g), wqt, wk, wvt, cos, sin, cos_t, sin_t)
            ya = _mla_attn(qt, k, vt, h1)
            yc = _sb_attn(h2, h2t, h1, tri)
            kbd, vbd = _mem_kv(mem[bi], l, wmk, wmv)
            yb, ym = _gmlp_mem(h1, h2, l, row3(sg_ln_g), row3(sg_ln_b), sg_w, sg_b_t, kbd, vbd)
            xf, xb = _out_ln(ya, yb, yc, ym, wo, l, xf, row3(ln_g), row3(ln_b), alpha)
        outs.append(xf)
    return outs[0][None] if b == 1 else jnp.stack(outs, axis=0)
```
